```python
import jax, jax.numpy as jnp
from jax import lax
import numpy as np

D_MODEL = 1024
BATCH = 32
SEQ = 2048
DEPTH = 4
DEC_BATCH = 8
DEC_SEQ = 32
PAST_LEN = 1024

CHUNK = 64
Q_BLOCK = 128
N_MIXERS = 4
ROPE_THETA = 10000.0
RMS_EPS = 1e-6
D_FF = 2816
H_A = 16
DH_A = 64
H_B = 8
DH_B = 128
H_IDX = 8
DH_IDX = 64
IDX_TOPK = 256
H_C = 16
Q_LORA = 256
KV_LORA = 128
QK_NOPE = 64
QK_ROPE = 32
V_C = 64
H_D = 16
DH_D = 64

kernel_name = "hybrid_stream_fox_dsa_mla_stickbreak"


def rmsnorm(x, g):
    xf = x.astype(jnp.float32)
    y = xf * lax.rsqrt(jnp.mean(xf * xf, axis=-1, keepdims=True) + RMS_EPS)
    return (y * g.astype(jnp.float32)).astype(x.dtype)


def rope(x, pos):
    half = x.shape[-1] // 2
    inv = ROPE_THETA ** (-jnp.arange(half, dtype=jnp.float32) / half)
    ang = pos.astype(jnp.float32)[:, None] * inv[None, :]
    cos = jnp.cos(ang)[:, None, :]
    sin = jnp.sin(ang)[:, None, :]
    xf = x.astype(jnp.float32)
    x1, x2 = xf[..., :half], xf[..., half:]
    return jnp.concatenate([x1 * cos - x2 * sin, x1 * sin + x2 * cos], axis=-1).astype(x.dtype)


def swiglu(h, w_in, w_out):
    up, gate = jnp.split(h @ w_in, 2, axis=-1)
    return (jax.nn.silu(gate) * up) @ w_out


def chunk_mask(qpos, kpos):
    return (kpos[None, :] // CHUNK) <= (qpos[:, None] // CHUNK)


def over_query_blocks(fn, q_arrays, qpos):
    n_q = qpos.shape[0]
    if n_q <= Q_BLOCK or n_q % Q_BLOCK:
        return fn(*q_arrays, qpos)
    nb = n_q // Q_BLOCK

    def split(a):
        return jnp.moveaxis(a.reshape(a.shape[0], nb, Q_BLOCK, *a.shape[2:]), 1, 0)

    xs = ([split(a) for a in q_arrays], qpos.reshape(nb, Q_BLOCK))
    out = lax.map(lambda t: fn(*t[0], t[1]), xs)
    out = jnp.moveaxis(out, 0, 1)
    return out.reshape(out.shape[0], n_q, *out.shape[3:])


def cat(past, new):
    return jnp.concatenate([past.astype(new.dtype), new], axis=1)


def fox_attend(q, c_q, qpos, k, v, c_k, kpos):
    s = jnp.einsum('bqhd,bkhd->bhqk', q, k).astype(jnp.float32) * (DH_A ** -0.5)
    s = s + jnp.transpose(c_q, (0, 2, 1))[..., :, None] - jnp.transpose(c_k, (0, 2, 1))[..., None, :]
    mask = kpos[None, :] <= qpos[:, None]
    p = jax.nn.softmax(jnp.where(mask, s, -jnp.inf), axis=-1)
    return jnp.einsum('bhqk,bkhd->bqhd', p.astype(v.dtype), v)


def fox_mixer(h, pos, past, w_in, b_f, w_out):
    B, T, _ = h.shape
    q, k, v, f = jnp.split(h @ w_in, np.cumsum([H_A * DH_A] * 3).tolist(), axis=-1)
    q = q.reshape(B, T, H_A, DH_A)
    k = k.reshape(B, T, H_A, DH_A)
    v = v.reshape(B, T, H_A, DH_A)
    logf = jax.nn.log_sigmoid((f + b_f).astype(jnp.float32))
    if past is None:
        k_all, v_all, logf_all = k, v, logf
    else:
        k_all, v_all = cat(past[0], k), cat(past[1], v)
        logf_all = jnp.concatenate([past[2].astype(jnp.float32), logf], axis=1)
    L = k_all.shape[1]
    c_all = jnp.cumsum(logf_all, axis=1)
    c_q = c_all[:, L - T:]
    kpos = jnp.arange(L, dtype=jnp.int32)
    o = over_query_blocks(lambda qb, cb, qp: fox_attend(qb, cb, qp, k_all, v_all, c_all, kpos), (q, c_q), pos)
    return o.reshape(B, T, H_A * DH_A) @ w_out, (k, v, logf)


def dsa_attend(q, qi, wi, qpos, k, v, ki, kpos, n_sel):
    dots = jnp.einsum('bqhd,bkd->bqhk', qi, ki).astype(jnp.float32)
    score = jnp.einsum('bqh,bqhk->bqk', wi.astype(jnp.float32), jax.nn.relu(dots))
    score = jnp.where(chunk_mask(qpos, kpos)[None], score, -jnp.inf)
    top_val, top_idx = lax.top_k(score, n_sel)
    valid = jnp.isfinite(top_val)
    gather = jax.vmap(lambda a, i: a[i])
    kg = gather(k, top_idx)
    vg = gather(v, top_idx)
    s = jnp.einsum('bqhd,bqkd->bhqk', q, kg).astype(jnp.float32) * (DH_B ** -0.5)
    s = jnp.where(jnp.transpose(valid, (0, 1, 2))[:, None], s, -jnp.inf)
    p = jax.nn.softmax(s, axis=-1)
    return jnp.einsum('bhqk,bqkd->bqhd', p.astype(vg.dtype), vg)


def dsa_mixer(h, pos, past, w_in, w_out):
    B, T, _ = h.shape
    sizes = [H_B * DH_B, DH_B, DH_B, H_IDX * DH_IDX, DH_IDX]
    q, k, v, qi, ki, wi = jnp.split(h @ w_in, np.cumsum(sizes).tolist(), axis=-1)
    q = rope(q.reshape(B, T, H_B, DH_B), pos)
    k = rope(k[:, :, None, :], pos)[:, :, 0, :]
    qi = rope(qi.reshape(B, T, H_IDX, DH_IDX), pos) * (DH_IDX ** -0.5)
    ki = rope(ki[:, :, None, :], pos)[:, :, 0, :]
    wi = wi * (H_IDX ** -0.5)
    if past is None:
        k_all, v_all, ki_all = k, v, ki
    else:
        k_all, v_all, ki_all = cat(past[0], k), cat(past[1], v), cat(past[2], ki)
    L = k_all.shape[1]
    n_sel = min(IDX_TOPK, L // 4)
    kpos = jnp.arange(L, dtype=jnp.int32)
    o = over_query_blocks(
        lambda qb, qib, wib, qp: dsa_attend(qb, qib, wib, qp, k_all, v_all, ki_all, kpos, n_sel),
        (q, qi, wi), pos)
    return o.reshape(B, T, H_B * DH_B) @ w_out, (k, v, ki)


def mla_attend(q_nope, q_pe, qpos, k_nope, k_pe, v, kpos):
    s = (jnp.einsum('bqhd,bkhd->bhqk', q_nope, k_nope)
         + jnp.einsum('bqhd,bkd->bhqk', q_pe, k_pe)).astype(jnp.float32) * ((QK_NOPE + QK_ROPE) ** -0.5)
    p = jax.nn.softmax(jnp.where(chunk_mask(qpos, kpos), s, -jnp.inf), axis=-1)
    return jnp.einsum('bhqk,bkhd->bqhd', p.astype(v.dtype), v)


def mla_mixer(h, pos, past, w_down, g_q, g_kv, w_uq, w_ukv, w_out):
    B, T, _ = h.shape
    cq, ckv, kpe = jnp.split(h @ w_down, [Q_LORA, Q_LORA + KV_LORA], axis=-1)
    cq = rmsnorm(cq, g_q)
    ckv = rmsnorm(ckv, g_kv)
    kpe = rope(kpe[:, :, None, :], pos)[:, :, 0, :]
    q = (cq @ w_uq).reshape(B, T, H_C, QK_NOPE + QK_ROPE)
    q_nope, q_pe = q[..., :QK_NOPE], rope(q[..., QK_NOPE:], pos)
    if past is None:
        ckv_all, kpe_all = ckv, kpe
    else:
        ckv_all, kpe_all = cat(past[0], ckv), cat(past[1], kpe)
    L = ckv_all.shape[1]
    kv = (ckv_all @ w_ukv).reshape(B, L, H_C, QK_NOPE + V_C)
    k_nope, v = kv[..., :QK_NOPE], kv[..., QK_NOPE:]
    kpos = jnp.arange(L, dtype=jnp.int32)
    o = over_query_blocks(lambda qn, qr, qp: mla_attend(qn, qr, qp, k_nope, kpe_all, v, kpos), (q_nope, q_pe), pos)
    return o.reshape(B, T, H_C * V_C) @ w_out, (ckv, kpe)


def sb_attend(q, qpos, k, v, kpos):
    z = jnp.einsum('bqhd,bkhd->bhqk', q, k).astype(jnp.float32) * (DH_D ** -0.5)
    mask = kpos[None, :] < qpos[:, None]
    log_beta = jax.nn.log_sigmoid(z)
    log_1mb = jnp.where(mask, jax.nn.log_sigmoid(-z), 0.0)
    tail = lax.cumsum(log_1mb, axis=3, reverse=True) - log_1mb
    a = jnp.where(mask, jnp.exp(log_beta + tail), 0.0)
    return jnp.einsum('bhqk,bkhd->bqhd', a.astype(v.dtype), v)


def sb_mixer(h, pos, past, w_in, w_out):
    B, T, _ = h.shape
    q, k, v = jnp.split(h @ w_in, 3, axis=-1)
    q = q.reshape(B, T, H_D, DH_D)
    k = k.reshape(B, T, H_D, DH_D)
    v = v.reshape(B, T, H_D, DH_D)
    if past is None:
        k_all, v_all = k, v
    else:
        k_all, v_all = cat(past[0], k), cat(past[1], v)
    kpos = jnp.arange(k_all.shape[1], dtype=jnp.int32)
    o = over_query_blocks(lambda qb, qp: sb_attend(qb, qp, k_all, v_all, kpos), (q,), pos)
    return o.reshape(B, T, H_D * DH_D) @ w_out, (k, v)


def setup_inputs(seed: int = 0) -> dict:
    key = jax.random.key(seed)
    ks = iter(jax.random.split(key, 40))

    def nrm(shape, scale=1.0):
        return jax.random.normal(next(ks), shape, dtype=jnp.float32) * scale

    def dense(shape):
        return nrm(shape, shape[-2] ** -0.5)

    d = D_MODEL
    return {
        'x_prompt': nrm((BATCH, SEQ, d)),
        'x_sample': nrm((DEC_BATCH, DEC_SEQ, d)),
        'cache_a_k': nrm((DEC_BATCH, PAST_LEN, H_A, DH_A)),
        'cache_a_v': nrm((DEC_BATCH, PAST_LEN, H_A, DH_A)),
        'cache_a_logf': jax.nn.log_sigmoid(2.0 + nrm((DEC_BATCH, PAST_LEN, H_A))),
        'cache_b_k': nrm((DEC_BATCH, PAST_LEN, DH_B)),
        'cache_b_v': nrm((DEC_BATCH, PAST_LEN, DH_B)),
        'cache_b_kidx': nrm((DEC_BATCH, PAST_LEN, DH_IDX)),
        'cache_c_ckv': nrm((DEC_BATCH, PAST_LEN, KV_LORA)),
        'cache_c_kpe': nrm((DEC_BATCH, PAST_LEN, QK_ROPE)),
        'cache_d_k': nrm((DEC_BATCH, PAST_LEN, H_D, DH_D)),
        'cache_d_v': nrm((DEC_BATCH, PAST_LEN, H_D, DH_D)),
        'norm_g': 1.0 + nrm((DEPTH, 3, d), 0.01),
        'final_g': 1.0 + nrm((d,), 0.01),
        'ffn1_w_in': dense((DEPTH, d, 2 * D_FF)),
        'ffn1_w_out': dense((DEPTH, D_FF, d)),
        'ffn2_w_in': dense((DEPTH, d, 2 * D_FF)),
        'ffn2_w_out': dense((DEPTH, D_FF, d)),
        'a_w_in': dense((d, 3 * H_A * DH_A + H_A)),
        'a_b_f': 2.0 + nrm((H_A,), 0.5),
        'a_w_out': dense((H_A * DH_A, d)),
        'b_w_in': dense((d, H_B * DH_B + 2 * DH_B + H_IDX * DH_IDX + DH_IDX + H_IDX)),
        'b_w_out': dense((H_B * DH_B, d)),
        'c_w_down': dense((d, Q_LORA + KV_LORA + QK_ROPE)),
        'c_g_q': 1.0 + nrm((Q_LORA,), 0.01),
        'c_g_kv': 1.0 + nrm((KV_LORA,), 0.01),
        'c_w_uq': dense((Q_LORA, H_C * (QK_NOPE + QK_ROPE))),
        'c_w_ukv': dense((KV_LORA, H_C * (QK_NOPE + V_C))),
        'c_w_out': dense((H_C * V_C, d)),
        'd_w_in': dense((d, 3 * H_D * DH_D)),
        'd_w_out': dense((H_D * DH_D, d)),
    }


def reference(x_prompt, x_sample, cache_a_k, cache_a_v, cache_a_logf, cache_b_k, cache_b_v, cache_b_kidx,
              cache_c_ckv, cache_c_kpe, cache_d_k, cache_d_v, norm_g, final_g,
              ffn1_w_in, ffn1_w_out, ffn2_w_in, ffn2_w_out, a_w_in, a_b_f, a_w_out, b_w_in, b_w_out,
              c_w_down, c_g_q, c_g_kv, c_w_uq, c_w_ukv, c_w_out, d_w_in, d_w_out):
    def trunk(x, past, past_len):
        T = x.shape[1]
        pos = past_len + jnp.arange(T, dtype=jnp.int32)
        rows = [None] * N_MIXERS
        for i in range(DEPTH):
            m = i % N_MIXERS
            x = x + 0.5 * swiglu(rmsnorm(x, norm_g[i, 0]), ffn1_w_in[i], ffn1_w_out[i])
            h = rmsnorm(x, norm_g[i, 1])
            pm = None if past is None else past[m]
            if m == 0:
                y, rows[m] = fox_mixer(h, pos, pm, a_w_in, a_b_f, a_w_out)
            elif m == 1:
                y, rows[m] = dsa_mixer(h, pos, pm, b_w_in, b_w_out)
            elif m == 2:
                y, rows[m] = mla_mixer(h, pos, pm, c_w_down, c_g_q, c_g_kv, c_w_uq, c_w_ukv, c_w_out)
            else:
                y, rows[m] = sb_mixer(h, pos, pm, d_w_in, d_w_out)
            x = x + y
            x = x + 0.5 * swiglu(rmsnorm(x, norm_g[i, 2]), ffn2_w_in[i], ffn2_w_out[i])
        return rmsnorm(x, final_g), rows

    y_prompt, prow = trunk(x_prompt, None, 0)
    past = ((cache_a_k, cache_a_v, cache_a_logf), (cache_b_k, cache_b_v, cache_b_kidx),
            (cache_c_ckv, cache_c_kpe), (cache_d_k, cache_d_v))
    y_sample, srow = trunk(x_sample, past, cache_a_k.shape[1])
    pa_k, pa_v, pa_logf = prow[0]
    pb_k, pb_v, pb_kidx = prow[1]
    pc_ckv, pc_kpe = prow[2]
    pd_k, pd_v = prow[3]
    sa_k, sa_v, sa_logf = srow[0]
    sb_k, sb_v, sb_kidx = srow[1]
    sc_ckv, sc_kpe = srow[2]
    sd_k, sd_v = srow[3]
    return (y_prompt, y_sample,
            pa_k, pa_v, pa_logf, pb_k, pb_v, pb_kidx, pc_ckv, pc_kpe, pd_k, pd_v,
            sa_k, sa_v, sa_logf, sb_k, sb_v, sb_kidx, sc_ckv, sc_kpe, sd_k, sd_v)
```

```python
import functools

import jax
import jax.numpy as jnp
import numpy as np
from jax import lax
from jax.experimental import pallas as pl
from jax.experimental.pallas import tpu as pltpu

F32 = jnp.float32
BF16 = jnp.bfloat16

D_MODEL = 1024
CHUNK_SHIFT = 6
ROPE_THETA = 10000.0
RMS_EPS = 1e-6
D_FF = 2816
H_A, DH_A = 16, 64
H_B, DH_B = 8, 128
H_IDX, DH_IDX = 8, 64
IDX_TOPK = 256
H_C, Q_LORA, KV_LORA, QK_NOPE, QK_ROPE, V_C = 16, 256, 128, 64, 32, 64
H_D, DH_D = 16, 64

LANES = 128
NEG = -1e30
VMEM_LIMIT = 56 * 1024 * 1024


def _cparams(sem):
    return pltpu.CompilerParams(dimension_semantics=sem, vmem_limit_bytes=VMEM_LIMIT)


def _dot(a, b):
    return jnp.dot(a, b, preferred_element_type=F32)


def _dot_nt(a, b):
    return lax.dot_general(a, b, (((1,), (1,)), ((), ())), preferred_element_type=F32)


def _rms(x, g):
    return x * lax.rsqrt(jnp.mean(x * x, axis=-1, keepdims=True) + RMS_EPS) * g


def _softplus_neg_abs(z):
    return jnp.log(1.0 + jnp.exp(-jnp.abs(z)))


def _log_sigmoid(z):
    return jnp.minimum(z, 0.0) - _softplus_neg_abs(z)


def _lane_lo():
    return lax.broadcasted_iota(jnp.int32, (1, LANES), 1) < (LANES // 2)


def _rope_slab(x, cos, sin, group):
    half = group // 2
    if group == LANES:
        partner = pltpu.roll(x, half, 1)
    else:
        lane = lax.broadcasted_iota(jnp.int32, (1, LANES), 1)
        first = (lane & (group - 1)) < half
        partner = jnp.where(first, pltpu.roll(x, LANES - half, 1), pltpu.roll(x, half, 1))
    return x * cos + partner * sin


def _ffn_kernel(x_ref, g_ref, wu_ref, wg_ref, wo_ref, *rest, nf, final):
    if final:
        fg_ref, o_ref, h_ref, acc_ref = rest
    else:
        o_ref, h_ref, acc_ref = rest
    f = pl.program_id(1)

    @pl.when(f == 0)
    def _():
        h_ref[...] = _rms(x_ref[...], g_ref[...]).astype(BF16)
        acc_ref[...] = jnp.zeros_like(acc_ref)

    h = h_ref[...]
    up = _dot(h, wu_ref[...])
    gate = _dot(h, wg_ref[...])
    act = (gate / (1.0 + jnp.exp(-gate)) * up).astype(BF16)
    acc_ref[...] += _dot(act, wo_ref[...])

    @pl.when(f == nf - 1)
    def _():
        y = x_ref[...] + 0.5 * acc_ref[...]
        if final:
            y = _rms(y, fg_ref[...])
        o_ref[...] = y


def _ffn(x, g, w_in, w_out, final_g=None, tm=512, tf=1408):
    m, d = x.shape
    dff = w_out.shape[0]
    tm = min(tm, m)
    nf = dff // tf
    final = final_g is not None
    in_specs = [
        pl.BlockSpec((tm, d), lambda i, f: (i, 0)),
        pl.BlockSpec((1, d), lambda i, f: (0, 0)),
        pl.BlockSpec((d, tf), lambda i, f: (0, f)),
        pl.BlockSpec((d, tf), lambda i, f: (0, f + nf)),
        pl.BlockSpec((tf, d), lambda i, f: (f, 0)),
    ]
    args = [x, g.reshape(1, d), w_in, w_in, w_out]
    if final:
        in_specs.append(pl.BlockSpec((1, d), lambda i, f: (0, 0)))
        args.append(final_g.reshape(1, d))
    return pl.pallas_call(
        functools.partial(_ffn_kernel, nf=nf, final=final),
        grid=(m // tm, nf),
        in_specs=in_specs,
        out_specs=pl.BlockSpec((tm, d), lambda i, f: (i, 0)),
        out_shape=jax.ShapeDtypeStruct((m, d), F32),
        scratch_shapes=[pltpu.VMEM((tm, d), BF16), pltpu.VMEM((tm, d), F32)],
        compiler_params=_cparams(("parallel", "arbitrary")),
        name="ffn",
    )(*args)


def _outproj_kernel(x_ref, o_ref, w_ref, y_ref):
    y_ref[...] = x_ref[...] + _dot(o_ref[...], w_ref[...])


def _outproj(x, o, w, tm=512):
    m, d = x.shape
    k = o.shape[1]
    tm = min(tm, m)
    return pl.pallas_call(
        _outproj_kernel,
        grid=(m // tm,),
        in_specs=[
            pl.BlockSpec((tm, d), lambda i: (i, 0)),
            pl.BlockSpec((tm, k), lambda i: (i, 0)),
            pl.BlockSpec((k, d), lambda i: (0, 0)),
        ],
        out_specs=pl.BlockSpec((tm, d), lambda i: (i, 0)),
        out_shape=jax.ShapeDtypeStruct((m, d), F32),
        compiler_params=_cparams(("parallel",)),
        name="outproj",
    )(x, o, w)


def _store_pad_heads64(dst_ref, q):
    lo = _lane_lo()
    for p in range(q.shape[1] // LANES):
        slab = q[:, p * LANES:(p + 1) * LANES]
        dst_ref[:, (2 * p) * LANES:(2 * p + 1) * LANES] = jnp.where(lo, slab, 0.0).astype(BF16)
        dst_ref[:, (2 * p + 1) * LANES:(2 * p + 2) * LANES] = jnp.where(lo, 0.0, slab).astype(BF16)


def _proj_qkv_kernel(x_ref, g_ref, wq_ref, wk_ref, wv_ref, *rest, scale, has_f):
    if has_f:
        wf_ref, wft_ref, bf_ref, bft_ref, qp_ref, k_ref, v_ref, kb_ref, vb_ref, lf_ref, lft_ref = rest
    else:
        qp_ref, k_ref, v_ref, kb_ref, vb_ref = rest
    h = _rms(x_ref[...], g_ref[...]).astype(BF16)
    _store_pad_heads64(qp_ref, _dot(h, wq_ref[...]) * scale)
    k = _dot(h, wk_ref[...])
    k_ref[...] = k
    kb_ref[...] = k.astype(BF16)
    v = _dot(h, wv_ref[...])
    v_ref[...] = v
    vb_ref[...] = v.astype(BF16)
    if has_f:
        lf_ref[...] = _log_sigmoid(_dot(h, wf_ref[...]) + bf_ref[...])
        lft_ref[...] = _log_sigmoid(_dot_nt(wft_ref[...], h) + bft_ref[...])


def _proj_qkv(x, g, wq, wk, wv, scale, wf=None, bf=None, tm=256):
    m, d = x.shape
    n = wq.shape[1]
    tm = min(tm, m)
    has_f = wf is not None
    row = lambda i: (i, 0)
    const = lambda i: (0, 0)
    in_specs = [pl.BlockSpec((tm, d), row), pl.BlockSpec((1, d), const)] + [pl.BlockSpec((d, n), const)] * 3
    args = [x, g.reshape(1, d), wq, wk, wv]
    out_specs = [pl.BlockSpec((tm, 2 * n), row)] + [pl.BlockSpec((tm, n), row)] * 4
    out_shape = [jax.ShapeDtypeStruct((m, 2 * n), BF16), jax.ShapeDtypeStruct((m, n), F32),
                 jax.ShapeDtypeStruct((m, n), F32), jax.ShapeDtypeStruct((m, n), BF16),
                 jax.ShapeDtypeStruct((m, n), BF16)]
    if has_f:
        nh = wf.shape[1]
        in_specs += [pl.BlockSpec((d, nh), const), pl.BlockSpec((nh, d), const),
                     pl.BlockSpec((1, nh), const), pl.BlockSpec((nh, 1), const)]
        args += [wf, wf.T, bf.reshape(1, nh), bf.reshape(nh, 1)]
        out_specs += [pl.BlockSpec((tm, nh), row), pl.BlockSpec((nh, tm), lambda i: (0, i))]
        out_shape += [jax.ShapeDtypeStruct((m, nh), F32), jax.ShapeDtypeStruct((nh, m), F32)]
    return pl.pallas_call(
        functools.partial(_proj_qkv_kernel, scale=scale, has_f=has_f),
        grid=(m // tm,),
        in_specs=in_specs,
        out_specs=out_specs,
        out_shape=out_shape,
        compiler_params=_cparams(("parallel",)),
        name="proj_qkv",
    )(*args)


def _split3(x):
    hi = x.astype(BF16)
    r1 = x - hi.astype(F32)
    mid = r1.astype(BF16)
    lo = (r1 - mid.astype(F32)).astype(BF16)
    return hi, mid, lo


def _cumsum_kernel(x_ref, c_ref, *, nblk):
    r = lax.broadcasted_iota(jnp.int32, (LANES, LANES), 0)
    c = lax.broadcasted_iota(jnp.int32, (LANES, LANES), 1)
    tri = jnp.where(r <= c, 1.0, 0.0).astype(BF16)
    carry = jnp.zeros((x_ref.shape[1], 1), F32)
    for b in range(nblk):
        hi, mid, lo = _split3(x_ref[0, :, b * LANES:(b + 1) * LANES])
        blk = (_dot(hi, tri) + _dot(mid, tri)) + _dot(lo, tri) + carry
        c_ref[0, :, b * LANES:(b + 1) * LANES] = blk
        carry = blk[:, LANES - 1:LANES]


def _cumsum_time(x):
    b, nh, l = x.shape
    spec = pl.BlockSpec((1, nh, l), lambda i: (i, 0, 0))
    return pl.pallas_call(
        functools.partial(_cumsum_kernel, nblk=l // LANES),
        grid=(b,),
        in_specs=[spec],
        out_specs=spec,
        out_shape=jax.ShapeDtypeStruct((b, nh, l), F32),
        compiler_params=_cparams(("parallel",)),
        name="cumsum_time",
    )(x)


def _last_block(i, tq, tk, q_off, chunked):
    q_last = q_off + (i + 1) * tq - 1
    if chunked:
        q_last = ((q_last >> CHUNK_SHIFT) << CHUNK_SHIFT) + (1 << CHUNK_SHIFT) - 1
    return q_last // tk


def _attn_kernel(q_ref, k_ref, v_ref, *rest, nh, k_per_head, bias, chunked, tq, tk, q_off, l_valid, nk):
    if bias:
        c_ref, o_ref, m_ref, l_ref, acc_ref = rest
    else:
        o_ref, m_ref, l_ref, acc_ref = rest
    i = pl.program_id(1)
    j = pl.program_id(2)

    @pl.when(j == 0)
    def _():
        m_ref[...] = jnp.full_like(m_ref, NEG)
        l_ref[...] = jnp.zeros_like(l_ref)
        acc_ref[...] = jnp.zeros_like(acc_ref)

    @pl.when(j <= jnp.minimum(_last_block(i, tq, tk, q_off, chunked), nk - 1))
    def _():
        qpos = q_off + i * tq + lax.broadcasted_iota(jnp.int32, (tq, tk), 0)
        kpos = j * tk + lax.broadcasted_iota(jnp.int32, (tq, tk), 1)
        if chunked:
            valid = (kpos >> CHUNK_SHIFT) <= (qpos >> CHUNK_SHIFT)
            if l_valid < nk * tk:
                valid = valid & (kpos < l_valid)
        else:
            valid = kpos <= qpos
        maskbias = jnp.where(valid, 0.0, NEG)
        lo = _lane_lo()
        for p in range(nh // 2):
            vs = v_ref[0, :, p * LANES:(p + 1) * LANES]
            alphas, pvs = [], []
            for e in range(2):
                h = 2 * p + e
                ks = h if k_per_head else p
                s = _dot_nt(q_ref[0, :, h * LANES:(h + 1) * LANES], k_ref[0, :, ks * LANES:(ks + 1) * LANES])
                if bias:
                    s = s - c_ref[0, h:h + 1, :]
                s = s + maskbias
                m_prev = m_ref[h]
                m_new = jnp.maximum(m_prev, jnp.max(s, axis=-1, keepdims=True))
                alpha = jnp.exp(m_prev - m_new)
                pe = jnp.exp(s - m_new)
                l_ref[h] = alpha * l_ref[h] + jnp.sum(pe, axis=-1, keepdims=True)
                m_ref[h] = m_new
                alphas.append(alpha)
                pvs.append(_dot(pe.astype(BF16), vs))
            acc_ref[p] = acc_ref[p] * jnp.where(lo, alphas[0], alphas[1]) + jnp.where(lo, pvs[0], pvs[1])

    @pl.when(j == nk - 1)
    def _():
        lo = _lane_lo()
        for p in range(nh // 2):
            denom = jnp.where(lo, l_ref[2 * p], l_ref[2 * p + 1])
            o_ref[0, :, p * LANES:(p + 1) * LANES] = (acc_ref[p] / denom).astype(BF16)


def _attention(q, k, v, c, *, nh, k_per_head, chunked, tq, tk, q_off, l_valid):
    b, t, _ = q.shape
    lpad = k.shape[1]
    nq, nk = t // tq, lpad // tk
    bias = c is not None

    def kv_map(bi, i, j):
        return (bi, jnp.minimum(j, jnp.minimum(_last_block(i, tq, tk, q_off, chunked), nk - 1)), 0)

    in_specs = [
        pl.BlockSpec((1, tq, q.shape[2]), lambda bi, i, j: (bi, i, 0)),
        pl.BlockSpec((1, tk, k.shape[2]), kv_map),
        pl.BlockSpec((1, tk, v.shape[2]), kv_map),
    ]
    args = [q, k, v]
    if bias:
        in_specs.append(pl.BlockSpec((1, nh, tk), lambda bi, i, j: (bi, 0, kv_map(bi, i, j)[1])))
        args.append(c)
    dv = v.shape[2]
    return pl.pallas_call(
        functools.partial(_attn_kernel, nh=nh, k_per_head=k_per_head, bias=bias, chunked=chunked,
                          tq=tq, tk=tk, q_off=q_off, l_valid=l_valid, nk=nk),
        grid=(b, nq, nk),
        in_specs=in_specs,
        out_specs=pl.BlockSpec((1, tq, dv), lambda bi, i, j: (bi, i, 0)),
        out_shape=jax.ShapeDtypeStruct((b, t, dv), BF16),
        scratch_shapes=[pltpu.VMEM((nh, tq, 1), F32), pltpu.VMEM((nh, tq, 1), F32),
                        pltpu.VMEM((nh // 2, tq, LANES), F32)],
        compiler_params=_cparams(("parallel", "parallel", "arbitrary")),
        name="attention",
    )(*args)


def _sb_kernel(q_ref, k_ref, v_ref, o_ref, tail_ref, acc_ref, *, nh, tq, tk, q_off, nk):
    i = pl.program_id(1)
    j = pl.program_id(2)

    @pl.when(j == 0)
    def _():
        tail_ref[...] = jnp.zeros_like(tail_ref)
        acc_ref[...] = jnp.zeros_like(acc_ref)

    last = jnp.minimum(_last_block(i, tq, tk, q_off, False), nk - 1)

    @pl.when(j <= last)
    def _():
        jj = last - j
        qpos = q_off + i * tq + lax.broadcasted_iota(jnp.int32, (tq, tk), 0)
        kpos = jj * tk + lax.broadcasted_iota(jnp.int32, (tq, tk), 1)
        valid = kpos < qpos
        r = lax.broadcasted_iota(jnp.int32, (tk, tk), 0)
        c = lax.broadcasted_iota(jnp.int32, (tk, tk), 1)
        after = jnp.where(r > c, 1.0, 0.0).astype(BF16)
        lo = _lane_lo()
        for p in range(nh // 2):
            ks = k_ref[0, :, p * LANES:(p + 1) * LANES]
            vs = v_ref[0, :, p * LANES:(p + 1) * LANES]
            pvs = []
            for e in range(2):
                h = 2 * p + e
                z = _dot_nt(q_ref[0, :, h * LANES:(h + 1) * LANES], ks)
                sp = _softplus_neg_abs(z)
                log_beta = jnp.minimum(z, 0.0) - sp
                log_1mb = jnp.where(valid, jnp.minimum(-z, 0.0) - sp, 0.0)
                hi = log_1mb.astype(BF16)
                rest = (log_1mb - hi.astype(F32)).astype(BF16)
                inner = _dot(hi, after) + _dot(rest, after)
                t_prev = tail_ref[h]
                a = jnp.where(valid, jnp.exp(log_beta + inner + t_prev), 0.0)
                tail_ref[h] = t_prev + jnp.sum(log_1mb, axis=-1, keepdims=True)
                pvs.append(_dot(a.astype(BF16), vs))
            acc_ref[p] += jnp.where(lo, pvs[0], pvs[1])

    @pl.when(j == nk - 1)
    def _():
        for p in range(nh // 2):
            o_ref[0, :, p * LANES:(p + 1) * LANES] = acc_ref[p].astype(BF16)


def _sb_attention(q, k, v, *, nh, tq, tk, q_off):
    b, t, _ = q.shape
    lpad = k.shape[1]
    nq, nk = t // tq, lpad // tk

    def kv_map(bi, i, j):
        last = jnp.minimum(_last_block(i, tq, tk, q_off, False), nk - 1)
        return (bi, jnp.maximum(last - j, 0), 0)

    dv = v.shape[2]
    return pl.pallas_call(
        functools.partial(_sb_kernel, nh=nh, tq=tq, tk=tk, q_off=q_off, nk=nk),
        grid=(b, nq, nk),
        in_specs=[
            pl.BlockSpec((1, tq, q.shape[2]), lambda bi, i, j: (bi, i, 0)),
            pl.BlockSpec((1, tk, k.shape[2]), kv_map),
            pl.BlockSpec((1, tk, dv), kv_map),
        ],
        out_specs=pl.BlockSpec((1, tq, dv), lambda bi, i, j: (bi, i, 0)),
        out_shape=jax.ShapeDtypeStruct((b, t, dv), BF16),
        scratch_shapes=[pltpu.VMEM((nh, tq, 1), F32), pltpu.VMEM((nh // 2, tq, LANES), F32)],
        compiler_params=_cparams(("parallel", "parallel", "arbitrary")),
        name="sb_attention",
    )(q, k, v)


def _proj_b_kernel(x_ref, g_ref, wq_ref, wk_ref, wv_ref, wqi_ref, wki_ref, wwi_ref,
                   c128_ref, s128_ref, c64_ref, s64_ref,
                   q_ref, k_ref, kb_ref, v_ref, vb_ref, qi_ref, ki_ref, kib_ref, wi_ref):
    h = _rms(x_ref[...], g_ref[...]).astype(BF16)
    c128, s128 = c128_ref[...], s128_ref[...]
    c64, s64 = c64_ref[...], s64_ref[...]
    q = _dot(h, wq_ref[...])
    for hh in range(H_B):
        sl = slice(hh * LANES, (hh + 1) * LANES)
        q_ref[:, sl] = (_rope_slab(q[:, sl], c128, s128, DH_B) * (DH_B ** -0.5)).astype(BF16)
    k = _rope_slab(_dot(h, wk_ref[...]), c128, s128, DH_B)
    k_ref[...] = k
    kb_ref[...] = k.astype(BF16)
    v = _dot(h, wv_ref[...])
    v_ref[...] = v
    vb_ref[...] = v.astype(BF16)
    qi = _dot(h, wqi_ref[...])
    lo = _lane_lo()
    for p in range(H_IDX // 2):
        slab = _rope_slab(qi[:, p * LANES:(p + 1) * LANES], c64, s64, DH_IDX) * (DH_IDX ** -0.5)
        qi_ref[:, (2 * p) * LANES:(2 * p + 1) * LANES] = jnp.where(lo, slab, 0.0).astype(BF16)
        qi_ref[:, (2 * p + 1) * LANES:(2 * p + 2) * LANES] = jnp.where(lo, 0.0, slab).astype(BF16)
    ki = _rope_slab(_dot(h, wki_ref[...]), c64, s64, DH_IDX)
    ki_ref[...] = ki
    kib_ref[...] = ki.astype(BF16)
    wi_ref[...] = _dot(h, wwi_ref[...]) * (H_IDX ** -0.5)


def _table_spec(tab, tm, t):
    if tab.shape[0] == tm:
        return pl.BlockSpec((tm, LANES), lambda i: (0, 0))
    period = t // tm
    return pl.BlockSpec((tm, LANES), lambda i: (i % period, 0))


def _fit_tables(tabs, tm, t):
    if tm > t:
        return [jnp.tile(tb, (tm // t, 1)) for tb in tabs]
    return tabs


def _proj_b(x, g, w, tabs, t, tm=256):
    m, d = x.shape
    tm = min(tm, m)
    tabs = _fit_tables(tabs, tm, t)
    row = lambda i: (i, 0)
    const = lambda i: (0, 0)
    wq, wk, wv, wqi, wki, wwi = w
    in_specs = [pl.BlockSpec((tm, d), row), pl.BlockSpec((1, d), const)]
    in_specs += [pl.BlockSpec(a.shape, const) for a in w]
    in_specs += [_table_spec(tb, tm, t) for tb in tabs]
    widths = [(H_B * DH_B, BF16), (LANES, F32), (LANES, BF16), (LANES, F32), (LANES, BF16),
              (2 * H_IDX * DH_IDX, BF16), (LANES, F32), (LANES, BF16), (H_IDX, F32)]
    return pl.pallas_call(
        _proj_b_kernel,
        grid=(m // tm,),
        in_specs=in_specs,
        out_specs=[pl.BlockSpec((tm, n), row) for n, _ in widths],
        out_shape=[jax.ShapeDtypeStruct((m, n), dt) for n, dt in widths],
        compiler_params=_cparams(("parallel",)),
        name="proj_b",
    )(x, g.reshape(1, d), *w, *tabs)


def _dsa_kernel(q_ref, qi_ref, wi_ref, k_ref, v_ref, ki_ref, o_ref, *, tq, lpad, l_valid, q_off, n_sel):
    i = pl.program_id(1)
    qpos = q_off + i * tq + lax.broadcasted_iota(jnp.int32, (tq, lpad), 0)
    kpos = lax.broadcasted_iota(jnp.int32, (tq, lpad), 1)
    valid = ((kpos >> CHUNK_SHIFT) <= (qpos >> CHUNK_SHIFT)) & (kpos < l_valid)

    ki = ki_ref[0]
    wi = wi_ref[0]
    score = jnp.zeros((tq, lpad), F32)
    for h in range(H_IDX):
        d = _dot_nt(qi_ref[0, :, h * LANES:(h + 1) * LANES], ki)
        score = score + wi[:, h:h + 1] * jnp.maximum(d, 0.0)
    score = jnp.where(valid, score + 0.0, -jnp.inf)

    bits = pltpu.bitcast(score, jnp.int32)
    key = jnp.where(bits < 0, bits ^ jnp.int32(0x7FFFFFFF), bits)

    int_min = jnp.int32(-2 ** 31)
    nsel_f = jnp.float32(n_sel)

    def count(pred):
        return jnp.sum(jnp.where(pred, 1.0, 0.0), axis=-1, keepdims=True)

    def thr_body(it, t):
        cand = t + lax.shift_left(jnp.int32(1), jnp.int32(31) - it)
        return jnp.where(count(key >= cand) >= nsel_f, cand, t)

    thr = lax.fori_loop(0, 32, thr_body, jnp.full((tq, 1), int_min, jnp.int32))

    need = nsel_f - count(key > thr)
    nbits = int(lpad).bit_length()

    def tie_body(it, bound):
        cand = bound + lax.shift_left(jnp.int32(1), jnp.int32(nbits - 1) - it)
        return jnp.where(count((key == thr) & (kpos < cand)) <= need, cand, bound)

    bound = lax.fori_loop(0, nbits, tie_body, jnp.zeros((tq, 1), jnp.int32))

    sel = ((key > thr) | ((key == thr) & (kpos < bound))) & valid
    selbias = jnp.where(sel, 0.0, NEG)

    k = k_ref[0]
    v = v_ref[0]
    for h in range(H_B):
        s = _dot_nt(q_ref[0, :, h * LANES:(h + 1) * LANES], k) + selbias
        m = jnp.max(s, axis=-1, keepdims=True)
        p = jnp.exp(s - m)
        l = jnp.sum(p, axis=-1, keepdims=True)
        o_ref[0, :, h * LANES:(h + 1) * LANES] = (_dot(p.astype(BF16), v) / l).astype(BF16)


def _dsa_attention(q, qi, wi, k, v, ki, *, tq, l_valid, q_off, n_sel):
    b, t, _ = q.shape
    lpad = k.shape[1]
    qmap = lambda bi, i: (bi, i, 0)
    kmap = lambda bi, i: (bi, 0, 0)
    return pl.pallas_call(
        functools.partial(_dsa_kernel, tq=tq, lpad=lpad, l_valid=l_valid, q_off=q_off, n_sel=n_sel),
        grid=(b, t // tq),
        in_specs=[
            pl.BlockSpec((1, tq, q.shape[2]), qmap),
            pl.BlockSpec((1, tq, qi.shape[2]), qmap),
            pl.BlockSpec((1, tq, wi.shape[2]), qmap),
            pl.BlockSpec((1, lpad, LANES), kmap),
            pl.BlockSpec((1, lpad, LANES), kmap),
            pl.BlockSpec((1, lpad, LANES), kmap),
        ],
        out_specs=pl.BlockSpec((1, tq, q.shape[2]), qmap),
        out_shape=jax.ShapeDtypeStruct(q.shape, BF16),
        compiler_params=_cparams(("parallel", "parallel")),
        name="dsa_attention",
    )(q, qi, wi, k, v, ki)


def _proj_c_kernel(x_ref, g_ref, wcq_ref, wckv_ref, wkpe_ref, gq_ref, gkv_ref, wuq_ref,
                   cq_ref, sq_ref, ck_ref, sk_ref, q_ref, ckv_ref, kpe_ref):
    h = _rms(x_ref[...], g_ref[...]).astype(BF16)
    cq = _rms(_dot(h, wcq_ref[...]), gq_ref[...]).astype(BF16)
    ckv_ref[...] = _rms(_dot(h, wckv_ref[...]), gkv_ref[...])
    kpe_ref[...] = _rope_slab(_dot(h, wkpe_ref[...]), ck_ref[...], sk_ref[...], QK_ROPE)
    q = _dot(cq, wuq_ref[...])
    cq_t, sq_t = cq_ref[...], sq_ref[...]
    scale = (QK_NOPE + QK_ROPE) ** -0.5
    for hh in range(H_C):
        sl = slice(hh * LANES, (hh + 1) * LANES)
        q_ref[:, sl] = (_rope_slab(q[:, sl], cq_t, sq_t, QK_ROPE) * scale).astype(BF16)


def _proj_c(x, g, w, gq, gkv, wuq, tabs, t, tm=256):
    m, d = x.shape
    tm = min(tm, m)
    tabs = _fit_tables(tabs, tm, t)
    row = lambda i: (i, 0)
    const = lambda i: (0, 0)
    in_specs = [pl.BlockSpec((tm, d), row), pl.BlockSpec((1, d), const)]
    in_specs += [pl.BlockSpec(a.shape, const) for a in w]
    in_specs += [pl.BlockSpec((1, Q_LORA), const), pl.BlockSpec((1, KV_LORA), const),
                 pl.BlockSpec(wuq.shape, const)]
    in_specs += [_table_spec(tb, tm, t) for tb in tabs]
    widths = [(H_C * LANES, BF16), (KV_LORA, F32), (LANES, F32)]
    return pl.pallas_call(
        _proj_c_kernel,
        grid=(m // tm,),
        in_specs=in_specs,
        out_specs=[pl.BlockSpec((tm, n), row) for n, _ in widths],
        out_shape=[jax.ShapeDtypeStruct((m, n), dt) for n, dt in widths],
        compiler_params=_cparams(("parallel",)),
        name="proj_c",
    )(x, g.reshape(1, d), *w, gq.reshape(1, Q_LORA), gkv.reshape(1, KV_LORA), wuq, *tabs)


def _kv_expand_kernel(ckv_ref, kpe_ref, wuk_ref, wuv_ref, place_ref, k_ref, v_ref):
    ckv = ckv_ref[...].astype(BF16)
    kpe = _dot(kpe_ref[...].astype(BF16), place_ref[...])
    kn = _dot(ckv, wuk_ref[...])
    for hh in range(H_C):
        sl = slice(hh * LANES, (hh + 1) * LANES)
        k_ref[:, sl] = (kn[:, sl] + kpe).astype(BF16)
    v_ref[...] = _dot(ckv, wuv_ref[...]).astype(BF16)


def _kv_expand(ckv, kpe, wuk, wuv, place, tm=256):
    m = ckv.shape[0]
    tm = min(tm, m)
    row = lambda i: (i, 0)
    const = lambda i: (0, 0)
    return pl.pallas_call(
        _kv_expand_kernel,
        grid=(m // tm,),
        in_specs=[pl.BlockSpec((tm, KV_LORA), row), pl.BlockSpec((tm, LANES), row),
                  pl.BlockSpec(wuk.shape, const), pl.BlockSpec(wuv.shape, const),
                  pl.BlockSpec(place.shape, const)],
        out_specs=[pl.BlockSpec((tm, H_C * LANES), row), pl.BlockSpec((tm, H_C * V_C), row)],
        out_shape=[jax.ShapeDtypeStruct((m, H_C * LANES), BF16), jax.ShapeDtypeStruct((m, H_C * V_C), BF16)],
        compiler_params=_cparams(("parallel",)),
        name="kv_expand",
    )(ckv, kpe, wuk, wuv, place)


def _rope_tables(pos, dim, group_offset=0, ones_below=0):
    half = dim // 2
    inv = ROPE_THETA ** (-jnp.arange(half, dtype=F32) / half)
    ang = pos.astype(F32)[:, None] * inv[None, :]
    cos, sin = jnp.cos(ang), jnp.sin(ang)
    cg = jnp.concatenate([cos, cos], axis=-1)
    sg = jnp.concatenate([-sin, sin], axis=-1)
    t = pos.shape[0]
    if group_offset == 0 and ones_below == 0:
        reps = LANES // dim
        return jnp.tile(cg, (1, reps)), jnp.tile(sg, (1, reps))
    c = jnp.zeros((t, LANES), F32).at[:, :ones_below].set(1.0).at[:, group_offset:group_offset + dim].set(cg)
    s = jnp.zeros((t, LANES), F32).at[:, group_offset:group_offset + dim].set(sg)
    return c, s


def _prep_weights(p):
    w = {}
    w["ffn1_in"] = p["ffn1_w_in"].astype(BF16)
    w["ffn1_out"] = p["ffn1_w_out"].astype(BF16)
    w["ffn2_in"] = p["ffn2_w_in"].astype(BF16)
    w["ffn2_out"] = p["ffn2_w_out"].astype(BF16)
    a = p["a_w_in"].astype(BF16)
    n = H_A * DH_A
    w["a"] = (a[:, :n], a[:, n:2 * n], a[:, 2 * n:3 * n], a[:, 3 * n:])
    w["a_out"] = p["a_w_out"].astype(BF16)
    b = p["b_w_in"].astype(BF16)
    offs = np.cumsum([0, H_B * DH_B, DH_B, DH_B, H_IDX * DH_IDX, DH_IDX, H_IDX]).tolist()
    bq, bk, bv, bqi, bki, bwi = [b[:, offs[i]:offs[i + 1]] for i in range(6)]
    w["b"] = (bq, bk, bv, bqi, jnp.concatenate([bki, bki], axis=1), bwi)
    w["b_out"] = p["b_w_out"].astype(BF16)
    c = p["c_w_down"].astype(BF16)
    d = c.shape[0]
    wkpe = jnp.zeros((d, LANES), BF16).at[:, :QK_ROPE].set(c[:, Q_LORA + KV_LORA:])
    w["c"] = (c[:, :Q_LORA], c[:, Q_LORA:Q_LORA + KV_LORA], wkpe)
    uq = p["c_w_uq"].astype(BF16).reshape(Q_LORA, H_C, QK_NOPE + QK_ROPE)
    uq = jnp.pad(uq, ((0, 0), (0, 0), (0, LANES - QK_NOPE - QK_ROPE)))
    w["c_uq"] = uq.reshape(Q_LORA, H_C * LANES)
    ukv = p["c_w_ukv"].astype(BF16).reshape(KV_LORA, H_C, QK_NOPE + V_C)
    uk = jnp.pad(ukv[:, :, :QK_NOPE], ((0, 0), (0, 0), (0, LANES - QK_NOPE)))
    w["c_uk"] = uk.reshape(KV_LORA, H_C * LANES)
    w["c_uv"] = ukv[:, :, QK_NOPE:].reshape(KV_LORA, H_C * V_C)
    w["c_place"] = jnp.zeros((LANES, LANES), BF16).at[jnp.arange(QK_ROPE), QK_NOPE + jnp.arange(QK_ROPE)].set(1.0)
    w["c_out"] = p["c_w_out"].astype(BF16)
    dw = p["d_w_in"].astype(BF16)
    n = H_D * DH_D
    w["d"] = (dw[:, :n], dw[:, n:2 * n], dw[:, 2 * n:])
    w["d_out"] = p["d_w_out"].astype(BF16)
    return w


def _round_up(x, m):
    return (x + m - 1) // m * m


def _with_past(past, new, lpad):
    parts = [new] if past is None else [past.astype(new.dtype), new]
    a = jnp.concatenate(parts, axis=1) if len(parts) > 1 else new
    if a.shape[1] < lpad:
        a = jnp.pad(a, ((0, 0), (0, lpad - a.shape[1]), (0, 0)))
    return a


def _trunk(x3, past, p, w):
    b, t, d = x3.shape
    m = b * t
    x = x3.reshape(m, d)
    past_len = 0 if past is None else past[0][0].shape[1]
    l_valid = past_len + t
    pos = past_len + jnp.arange(t, dtype=jnp.int32)
    if past is None:
        tq = tk = min(256, t)
        lpad = _round_up(l_valid, tk)
    else:
        tq = t
        lpad = tk = _round_up(l_valid, LANES)
    tq_dsa = min(128, t)
    norm_g = p["norm_g"]
    rows = {}

    def r3(a):
        return a.reshape(b, t, a.shape[-1])

    def pst(idx, j, width):
        return None if past is None else past[idx][j].reshape(b, past_len, width)

    for i in range(4):
        x = _ffn(x, norm_g[i, 0], w["ffn1_in"][i], w["ffn1_out"][i])
        g = norm_g[i, 1]
        if i == 0:
            wq, wk, wv, wf = w["a"]
            qp, k, v, kb, vb, lf, lft = _proj_qkv(x, g, wq, wk, wv, DH_A ** -0.5, wf, p["a_b_f"])
            rows["a"] = (k.reshape(b, t, H_A, DH_A), v.reshape(b, t, H_A, DH_A), lf.reshape(b, t, H_A))
            lft = jnp.transpose(lft.reshape(H_A, b, t), (1, 0, 2))
            if past is not None:
                lft = jnp.concatenate([jnp.transpose(past[0][2].astype(F32), (0, 2, 1)), lft], axis=2)
            lft = jnp.pad(lft, ((0, 0), (0, 0), (0, lpad - l_valid)))
            c = _cumsum_time(lft)
            o = _attention(r3(qp), _with_past(pst(0, 0, H_A * DH_A), r3(kb), lpad),
                           _with_past(pst(0, 1, H_A * DH_A), r3(vb), lpad), c,
                           nh=H_A, k_per_head=False, chunked=False, tq=tq, tk=tk, q_off=past_len, l_valid=l_valid)
            x = _outproj(x, o.reshape(m, -1), w["a_out"])
        elif i == 1:
            tabs = list(_rope_tables(pos, DH_B)) + list(_rope_tables(pos, DH_IDX))
            q, k, kb, v, vb, qi, ki, kib, wi = _proj_b(x, g, w["b"], tabs, t)
            rows["b"] = (r3(k), r3(v), r3(ki)[:, :, :DH_IDX])
            pki = None if past is None else jnp.concatenate([past[1][2], past[1][2]], axis=-1)
            n_sel = min(IDX_TOPK, l_valid // 4)
            o = _dsa_attention(r3(q), r3(qi), r3(wi), _with_past(pst(1, 0, DH_B), r3(kb), lpad),
                               _with_past(pst(1, 1, DH_B), r3(vb), lpad), _with_past(pki, r3(kib), lpad),
                               tq=tq_dsa, l_valid=l_valid, q_off=past_len, n_sel=n_sel)
            x = _outproj(x, o.reshape(m, -1), w["b_out"])
        elif i == 2:
            tabs = list(_rope_tables(pos, QK_ROPE, QK_NOPE, QK_NOPE)) + list(_rope_tables(pos, QK_ROPE))
            q, ckv, kpe = _proj_c(x, g, w["c"], p["c_g_q"], p["c_g_kv"], w["c_uq"], tabs, t)
            rows["c"] = (r3(ckv), r3(kpe)[:, :, :QK_ROPE])
            if past is None:
                ckv_all, kpe_all = r3(ckv), r3(kpe)
            else:
                ckv_all = jnp.concatenate([past[2][0].astype(F32), r3(ckv)], axis=1)
                kpe_past = jnp.pad(past[2][1].astype(F32), ((0, 0), (0, 0), (0, LANES - QK_ROPE)))
                kpe_all = jnp.concatenate([kpe_past, r3(kpe)], axis=1)
            ckv_all = _with_past(None, ckv_all, lpad)
            kpe_all = _with_past(None, kpe_all, lpad)
            kc, vc = _kv_expand(ckv_all.reshape(b * lpad, KV_LORA), kpe_all.reshape(b * lpad, LANES),
                                w["c_uk"], w["c_uv"], w["c_place"])
            o = _attention(r3(q), kc.reshape(b, lpad, -1), vc.reshape(b, lpad, -1), None,
                           nh=H_C, k_per_head=True, chunked=True, tq=tq, tk=tk, q_off=past_len, l_valid=l_valid)
            x = _outproj(x, o.reshape(m, -1), w["c_out"])
        else:
            wq, wk, wv = w["d"]
            qp, k, v, kb, vb = _proj_qkv(x, g, wq, wk, wv, DH_D ** -0.5)
            rows["d"] = (k.reshape(b, t, H_D, DH_D), v.reshape(b, t, H_D, DH_D))
            o = _sb_attention(r3(qp), _with_past(pst(3, 0, H_D * DH_D), r3(kb), lpad),
                              _with_past(pst(3, 1, H_D * DH_D), r3(vb), lpad),
                              nh=H_D, tq=tq, tk=tk, q_off=past_len)
            x = _outproj(x, o.reshape(m, -1), w["d_out"])
        x = _ffn(x, norm_g[i, 2], w["ffn2_in"][i], w["ffn2_out"][i], p["final_g"] if i == 3 else None)
    return x.reshape(b, t, d), rows


def kernel(x_prompt, x_sample, cache_a_k, cache_a_v, cache_a_logf, cache_b_k, cache_b_v, cache_b_kidx,
           cache_c_ckv, cache_c_kpe, cache_d_k, cache_d_v, norm_g, final_g,
           ffn1_w_in, ffn1_w_out, ffn2_w_in, ffn2_w_out, a_w_in, a_b_f, a_w_out, b_w_in, b_w_out,
           c_w_down, c_g_q, c_g_kv, c_w_uq, c_w_ukv, c_w_out, d_w_in, d_w_out):
    p = dict(norm_g=norm_g, final_g=final_g, ffn1_w_in=ffn1_w_in, ffn1_w_out=ffn1_w_out,
             ffn2_w_in=ffn2_w_in, ffn2_w_out=ffn2_w_out, a_w_in=a_w_in, a_b_f=a_b_f, a_w_out=a_w_out,
             b_w_in=b_w_in, b_w_out=b_w_out, c_w_down=c_w_down, c_g_q=c_g_q, c_g_kv=c_g_kv,
             c_w_uq=c_w_uq, c_w_ukv=c_w_ukv, c_w_out=c_w_out, d_w_in=d_w_in, d_w_out=d_w_out)
    w = _prep_weights(p)
    y_prompt, pr = _trunk(x_prompt, None, p, w)
    past = ((cache_a_k, cache_a_v, cache_a_logf), (cache_b_k, cache_b_v, cache_b_kidx),
            (cache_c_ckv, cache_c_kpe), (cache_d_k, cache_d_v))
    y_sample, sr = _trunk(x_sample, past, p, w)
    return (y_prompt, y_sample, *pr["a"], *pr["b"], *pr["c"], *pr["d"],
            *sr["a"], *sr["b"], *sr["c"], *sr["d"])
```

```python
import functools

import jax
import jax.numpy as jnp
import numpy as np
from jax import lax
from jax.experimental import pallas as pl
from jax.experimental.pallas import tpu as pltpu

F32 = jnp.float32
BF16 = jnp.bfloat16

D_MODEL = 1024
CHUNK_SHIFT = 6
ROPE_THETA = 10000.0
RMS_EPS = 1e-6
D_FF = 2816
H_A, DH_A = 16, 64
H_B, DH_B = 8, 128
H_IDX, DH_IDX = 8, 64
IDX_TOPK = 256
H_C, Q_LORA, KV_LORA, QK_NOPE, QK_ROPE, V_C = 16, 256, 128, 64, 32, 64
H_D, DH_D = 16, 64

LANES = 128
NEG = -1e30
LOG2E = 1.4426950408889634
EXP2_ZERO = -150.0
VROWS = 80
VMEM_LIMIT = 56 * 1024 * 1024


def _cparams(sem):
    return pltpu.CompilerParams(dimension_semantics=sem, vmem_limit_bytes=VMEM_LIMIT)


def _dot(a, b):
    return jnp.dot(a, b, preferred_element_type=F32)


def _dot_nt(a, b):
    return lax.dot_general(a, b, (((1,), (1,)), ((), ())), preferred_element_type=F32)


def _rms(x, g):
    return x * lax.rsqrt(jnp.mean(x * x, axis=-1, keepdims=True) + RMS_EPS) * g


def _softplus_neg_abs(z):
    return jnp.log(1.0 + jnp.exp(-jnp.abs(z)))


def _log_sigmoid(z):
    return jnp.minimum(z, 0.0) - _softplus_neg_abs(z)


def _rope_slab(x, cos, sin, group):
    half = group // 2
    if group == LANES:
        partner = pltpu.roll(x, half, 1)
    else:
        lane = lax.broadcasted_iota(jnp.int32, (1, LANES), 1)
        first = (lane & (group - 1)) < half
        partner = jnp.where(first, pltpu.roll(x, LANES - half, 1), pltpu.roll(x, half, 1))
    return x * cos + partner * sin


def _store_aug_values(vt_ref, vt):
    dv = LANES // 2
    n = vt.shape[1]
    extra = jnp.where(lax.broadcasted_iota(jnp.int32, (VROWS - dv, n), 0) == 0, 1.0, 0.0).astype(BF16)
    for hh in range(vt.shape[0] // dv):
        vt_ref[0, hh * VROWS:hh * VROWS + dv, :] = vt[hh * dv:(hh + 1) * dv, :]
        vt_ref[0, hh * VROWS + dv:(hh + 1) * VROWS, :] = extra


def _rope_rows(x1, x2, cos, sin):
    return x1 * cos - x2 * sin, x1 * sin + x2 * cos


def _ffn_kernel(x_ref, g_ref, wu_ref, wg_ref, wo_ref, *rest, nf, final):
    if final:
        fg_ref, o_ref, h_ref, acc_ref = rest
    else:
        o_ref, h_ref, acc_ref = rest
    f = pl.program_id(1)

    @pl.when(f == 0)
    def _():
        h_ref[...] = _rms(x_ref[...], g_ref[...]).astype(BF16)
        acc_ref[...] = jnp.zeros_like(acc_ref)

    h = h_ref[...]
    up = _dot(h, wu_ref[...])
    gate = _dot(h, wg_ref[...])
    act = (gate / (1.0 + jnp.exp(-gate)) * up).astype(BF16)
    acc_ref[...] += _dot(act, wo_ref[...])

    @pl.when(f == nf - 1)
    def _():
        y = x_ref[...] + 0.5 * acc_ref[...]
        if final:
            y = _rms(y, fg_ref[...])
        o_ref[...] = y


def _ffn(x, g, w_in, w_out, final_g=None, tm=512, tf=1408):
    m, d = x.shape
    dff = w_out.shape[0]
    tm = min(tm, m)
    nf = dff // tf
    final = final_g is not None
    in_specs = [
        pl.BlockSpec((tm, d), lambda i, f: (i, 0)),
        pl.BlockSpec((1, d), lambda i, f: (0, 0)),
        pl.BlockSpec((d, tf), lambda i, f: (0, f)),
        pl.BlockSpec((d, tf), lambda i, f: (0, f + nf)),
        pl.BlockSpec((tf, d), lambda i, f: (f, 0)),
    ]
    args = [x, g.reshape(1, d), w_in, w_in, w_out]
    if final:
        in_specs.append(pl.BlockSpec((1, d), lambda i, f: (0, 0)))
        args.append(final_g.reshape(1, d))
    return pl.pallas_call(
        functools.partial(_ffn_kernel, nf=nf, final=final),
        grid=(m // tm, nf),
        in_specs=in_specs,
        out_specs=pl.BlockSpec((tm, d), lambda i, f: (i, 0)),
        out_shape=jax.ShapeDtypeStruct((m, d), F32),
        scratch_shapes=[pltpu.VMEM((tm, d), BF16), pltpu.VMEM((tm, d), F32)],
        compiler_params=_cparams(("parallel", "arbitrary")),
        name="ffn",
    )(*args)


def _outproj_kernel(x_ref, o_ref, w_ref, y_ref):
    y_ref[...] = x_ref[...] + _dot(o_ref[...], w_ref[...])


def _outproj(x, o, w, tm=512):
    m, d = x.shape
    k = o.shape[1]
    tm = min(tm, m)
    return pl.pallas_call(
        _outproj_kernel,
        grid=(m // tm,),
        in_specs=[
            pl.BlockSpec((tm, d), lambda i: (i, 0)),
            pl.BlockSpec((tm, k), lambda i: (i, 0)),
            pl.BlockSpec((k, d), lambda i: (0, 0)),
        ],
        out_specs=pl.BlockSpec((tm, d), lambda i: (i, 0)),
        out_shape=jax.ShapeDtypeStruct((m, d), F32),
        compiler_params=_cparams(("parallel",)),
        name="outproj",
    )(x, o, w)


def _row(i):
    return (i, 0)


def _const(i):
    return (0, 0)


def _feat_time(nt):
    return lambda i: (i // nt, 0, i % nt)


def _time_tab(nt):
    return lambda i: (0, i % nt)


def _row_tab(nt):
    return lambda i: (i % nt, 0)


def _proj_qkv_kernel(x_ref, g_ref, wqt_ref, wk_ref, wv_ref, wvt_ref, *rest, scale, has_f):
    if has_f:
        wf_ref, bf_ref, qt_ref, k_ref, kb_ref, v_ref, vt_ref, lf_ref = rest
    else:
        qt_ref, k_ref, kb_ref, v_ref, vt_ref = rest
    h = _rms(x_ref[...], g_ref[...]).astype(BF16)
    qt = (_dot_nt(wqt_ref[...], h) * scale).astype(BF16)
    half = LANES // 2
    zeros = jnp.zeros((half, qt.shape[1]), BF16)
    for hh in range(qt.shape[0] // half):
        base = hh * LANES
        own = slice(base + (hh % 2) * half, base + (hh % 2) * half + half)
        other = slice(base + (1 - hh % 2) * half, base + (1 - hh % 2) * half + half)
        qt_ref[0, own, :] = qt[hh * half:(hh + 1) * half, :]
        qt_ref[0, other, :] = zeros
    k = _dot(h, wk_ref[...])
    k_ref[...] = k
    kb_ref[...] = k.astype(BF16)
    v_ref[...] = _dot(h, wv_ref[...])
    vt = _dot_nt(wvt_ref[...], h).astype(BF16)
    if has_f:
        _store_aug_values(vt_ref, vt)
        lf_ref[...] = _log_sigmoid(_dot(h, wf_ref[...]) + bf_ref[...])
    else:
        vt_ref[0] = vt


def _proj_qkv(x, g, wqt, wk, wv, wvt, scale, b, t, wf=None, bf=None, tm=256):
    m, d = x.shape
    n = wk.shape[1]
    tm = min(tm, t)
    nt = t // tm
    has_f = wf is not None
    nvt = n // (LANES // 2) * VROWS if has_f else n
    in_specs = [pl.BlockSpec((tm, d), _row), pl.BlockSpec((1, d), _const)] + [pl.BlockSpec((d, n), _const)] * 4
    args = [x, g.reshape(1, d), wqt, wk, wv, wvt]
    out_specs = [pl.BlockSpec((1, 2 * n, tm), _feat_time(nt)), pl.BlockSpec((tm, n), _row),
                 pl.BlockSpec((tm, n), _row), pl.BlockSpec((tm, n), _row), pl.BlockSpec((1, nvt, tm), _feat_time(nt))]
    out_shape = [jax.ShapeDtypeStruct((b, 2 * n, t), BF16), jax.ShapeDtypeStruct((m, n), F32),
                 jax.ShapeDtypeStruct((m, n), BF16), jax.ShapeDtypeStruct((m, n), F32),
                 jax.ShapeDtypeStruct((b, nvt, t), BF16)]
    if has_f:
        nh = wf.shape[1]
        in_specs += [pl.BlockSpec((d, nh), _const), pl.BlockSpec((1, nh), _const)]
        args += [wf, bf.reshape(1, nh)]
        out_specs.append(pl.BlockSpec((tm, nh), _row))
        out_shape.append(jax.ShapeDtypeStruct((m, nh), F32))
    return pl.pallas_call(
        functools.partial(_proj_qkv_kernel, scale=scale, has_f=has_f),
        grid=(m // tm,),
        in_specs=in_specs,
        out_specs=out_specs,
        out_shape=out_shape,
        compiler_params=_cparams(("parallel",)),
        name="proj_qkv",
    )(*args)


def _split3(x):
    hi = x.astype(BF16)
    r1 = x - hi.astype(F32)
    mid = r1.astype(BF16)
    lo = (r1 - mid.astype(F32)).astype(BF16)
    return hi, mid, lo


def _cumsum_kernel(x_ref, c_ref, *, nblk):
    r = lax.broadcasted_iota(jnp.int32, (LANES, LANES), 0)
    c = lax.broadcasted_iota(jnp.int32, (LANES, LANES), 1)
    tri = jnp.where(c <= r, 1.0, 0.0).astype(BF16)
    carry = jnp.zeros((1, x_ref.shape[2]), F32)
    for b in range(nblk):
        hi, mid, lo = _split3(x_ref[0, b * LANES:(b + 1) * LANES, :])
        blk = (_dot(tri, hi) + _dot(tri, mid)) + _dot(tri, lo) + carry
        c_ref[0, b * LANES:(b + 1) * LANES, :] = blk
        carry = blk[LANES - 1:LANES, :]


def _cumsum_time(x):
    b, l, nh = x.shape
    spec = pl.BlockSpec((1, l, nh), lambda i: (i, 0, 0))
    return pl.pallas_call(
        functools.partial(_cumsum_kernel, nblk=l // LANES),
        grid=(b,),
        in_specs=[spec],
        out_specs=spec,
        out_shape=jax.ShapeDtypeStruct((b, l, nh), F32),
        compiler_params=_cparams(("parallel",)),
        name="cumsum_time",
    )(x)


def _last_block(i, tq, tk, q_off, chunked, nk):
    q_last = q_off + (i + 1) * tq - 1
    if chunked:
        q_last = ((q_last >> CHUNK_SHIFT) << CHUNK_SHIFT) + (1 << CHUNK_SHIFT) - 1
    return jnp.minimum(q_last // tk, nk - 1)


def _attn_kernel(qt_ref, k_ref, vt_ref, *rest, nh, k_per_head, bias, chunked, tq, tk, q_off, l_valid, nk):
    if bias:
        c_ref, o_ref, m_ref, acc_ref = rest
    else:
        o_ref, m_ref, acc_ref = rest
    i = pl.program_id(1)
    j = pl.program_id(2)
    dv = LANES // 2

    @pl.when(j == 0)
    def _():
        m_ref[...] = jnp.full_like(m_ref, NEG)
        acc_ref[...] = jnp.zeros_like(acc_ref)

    q_first = q_off + i * tq
    k_last = j * tk + tk - 1
    if chunked:
        full = ((k_last >> CHUNK_SHIFT) <= (q_first >> CHUNK_SHIFT)) & (k_last < l_valid)
    else:
        full = k_last <= q_first
    active = j <= _last_block(i, tq, tk, q_off, chunked, nk)

    def step(masked):
        if masked:
            kpos = j * tk + lax.broadcasted_iota(jnp.int32, (tk, tq), 0)
            qpos = q_first + lax.broadcasted_iota(jnp.int32, (tk, tq), 1)
            if chunked:
                valid = ((kpos >> CHUNK_SHIFT) <= (qpos >> CHUNK_SHIFT)) & (kpos < l_valid)
            else:
                valid = kpos <= qpos
            maskbias = jnp.where(valid, 0.0, NEG)
        def qk(h):
            ks = h if k_per_head else h // 2
            return _dot(k_ref[0, :, ks * LANES:(ks + 1) * LANES], qt_ref[0, h * LANES:(h + 1) * LANES, :])

        s_next = qk(0)
        for h in range(nh):
            s = s_next
            if h + 1 < nh:
                s_next = qk(h + 1)
            if bias:
                s = s - c_ref[0, :, h:h + 1] * LOG2E
            if masked:
                s = s + maskbias
            m_prev = m_ref[h:h + 1, :]
            m_new = jnp.maximum(m_prev, jnp.max(s, axis=0, keepdims=True))
            alpha = jnp.exp2(m_prev - m_new)
            p = jnp.exp2(s - m_new)
            m_ref[h:h + 1, :] = m_new
            rows = slice(h * VROWS, (h + 1) * VROWS)
            acc_ref[rows, :] = alpha * acc_ref[rows, :] + _dot(vt_ref[0, rows, :], p.astype(BF16))

    @pl.when(active & full)
    def _():
        step(False)

    @pl.when(active & jnp.logical_not(full))
    def _():
        step(True)

    @pl.when(j == nk - 1)
    def _():
        for p in range(nh // 2):
            parts = [acc_ref[h * VROWS:h * VROWS + dv, :] / acc_ref[h * VROWS + dv:h * VROWS + dv + 1, :]
                     for h in (2 * p, 2 * p + 1)]
            o_ref[0, :, p * LANES:(p + 1) * LANES] = jnp.concatenate(parts, axis=0).T.astype(BF16)


def _attention(qt, k, vt, c, *, nh, k_per_head, chunked, tq, tk, q_off, l_valid):
    b, _, t = qt.shape
    lpad = k.shape[1]
    nq, nk = t // tq, lpad // tk
    bias = c is not None

    def kblk(i, j):
        return jnp.minimum(j, _last_block(i, tq, tk, q_off, chunked, nk))

    in_specs = [
        pl.BlockSpec((1, qt.shape[1], tq), lambda bi, i, j: (bi, 0, i)),
        pl.BlockSpec((1, tk, k.shape[2]), lambda bi, i, j: (bi, kblk(i, j), 0)),
        pl.BlockSpec((1, vt.shape[1], tk), lambda bi, i, j: (bi, 0, kblk(i, j))),
    ]
    args = [qt, k, vt]
    if bias:
        in_specs.append(pl.BlockSpec((1, tk, nh), lambda bi, i, j: (bi, kblk(i, j), 0)))
        args.append(c)
    dv = nh * (LANES // 2)
    return pl.pallas_call(
        functools.partial(_attn_kernel, nh=nh, k_per_head=k_per_head, bias=bias, chunked=chunked,
                          tq=tq, tk=tk, q_off=q_off, l_valid=l_valid, nk=nk),
        grid=(b, nq, nk),
        in_specs=in_specs,
        out_specs=pl.BlockSpec((1, tq, dv), lambda bi, i, j: (bi, i, 0)),
        out_shape=jax.ShapeDtypeStruct((b, t, dv), BF16),
        scratch_shapes=[pltpu.VMEM((nh, tq), F32), pltpu.VMEM((vt.shape[1], tq), F32)],
        compiler_params=_cparams(("parallel", "parallel", "arbitrary")),
        name="attention",
    )(*args)


def _sb_kernel(qt_ref, k_ref, vt_ref, o_ref, tail_ref, acc_ref, *, nh, tq, tk, q_off, nk):
    i = pl.program_id(1)
    j = pl.program_id(2)
    dv = LANES // 2

    @pl.when(j == 0)
    def _():
        tail_ref[...] = jnp.zeros_like(tail_ref)
        acc_ref[...] = jnp.zeros_like(acc_ref)

    last = _last_block(i, tq, tk, q_off, False, nk)
    jj = last - j
    q_first = q_off + i * tq
    full = jj * tk + tk - 1 < q_first
    active = (j <= last) & (jnp.max(tail_ref[...]) > EXP2_ZERO)

    def step(masked):
        r = lax.broadcasted_iota(jnp.int32, (tk + 16, tk), 0)
        c = lax.broadcasted_iota(jnp.int32, (tk + 16, tk), 1)
        after = jnp.where(((r < tk) & (c > r)) | (r == tk), 1.0, 0.0).astype(BF16)
        if masked:
            kpos = jj * tk + lax.broadcasted_iota(jnp.int32, (tk, tq), 0)
            qpos = q_first + lax.broadcasted_iota(jnp.int32, (tk, tq), 1)
            valid = kpos < qpos

        def qk(h):
            return _dot(k_ref[0, :, (h // 2) * LANES:(h // 2 + 1) * LANES], qt_ref[0, h * LANES:(h + 1) * LANES, :])

        def logs(z):
            log_beta = jnp.minimum(z, 0.0) - jnp.log2(1.0 + jnp.exp2(-jnp.abs(z)))
            log_1mb = log_beta - z
            if masked:
                log_1mb = jnp.where(valid, log_1mb, 0.0)
            hi = log_1mb.astype(BF16)
            rest = (log_1mb - hi.astype(F32)).astype(BF16)
            return log_beta, _dot(after, hi) + _dot(after, rest)

        def finish(h, log_beta, sums):
            t_prev = tail_ref[h:h + 1, :]
            a = jnp.exp2(log_beta + sums[:tk, :] + t_prev)
            if masked:
                a = jnp.where(valid, a, 0.0)
            tail_ref[h:h + 1, :] = t_prev + sums[tk:tk + 1, :]
            rows = slice(h * dv, (h + 1) * dv)
            acc_ref[rows, :] += _dot(vt_ref[0, rows, :], a.astype(BF16))

        z_next = qk(0)
        pending = None
        for h in range(nh):
            z = z_next
            if h + 1 < nh:
                z_next = qk(h + 1)
            cur = logs(z)
            if pending is not None:
                finish(h - 1, *pending)
            pending = cur
        finish(nh - 1, *pending)

    @pl.when(active & full)
    def _():
        step(False)

    @pl.when(active & jnp.logical_not(full))
    def _():
        step(True)

    @pl.when(j == nk - 1)
    def _():
        for p in range(nh // 2):
            o_ref[0, :, p * LANES:(p + 1) * LANES] = acc_ref[p * LANES:(p + 1) * LANES, :].T.astype(BF16)


def _sb_attention(qt, k, vt, *, nh, tq, tk, q_off):
    b, _, t = qt.shape
    lpad = k.shape[1]
    nq, nk = t // tq, lpad // tk

    def kblk(i, j):
        return jnp.maximum(_last_block(i, tq, tk, q_off, False, nk) - j, 0)

    dv = vt.shape[1]
    return pl.pallas_call(
        functools.partial(_sb_kernel, nh=nh, tq=tq, tk=tk, q_off=q_off, nk=nk),
        grid=(b, nq, nk),
        in_specs=[
            pl.BlockSpec((1, qt.shape[1], tq), lambda bi, i, j: (bi, 0, i)),
            pl.BlockSpec((1, tk, k.shape[2]), lambda bi, i, j: (bi, kblk(i, j), 0)),
            pl.BlockSpec((1, dv, tk), lambda bi, i, j: (bi, 0, kblk(i, j))),
        ],
        out_specs=pl.BlockSpec((1, tq, dv), lambda bi, i, j: (bi, i, 0)),
        out_shape=jax.ShapeDtypeStruct((b, t, dv), BF16),
        scratch_shapes=[pltpu.VMEM((nh, tq), F32), pltpu.VMEM((dv, tq), F32)],
        compiler_params=_cparams(("parallel", "parallel", "arbitrary")),
        name="sb_attention",
    )(qt, k, vt)


def _proj_b_kernel(x_ref, g_ref, wqt_ref, wk_ref, wv_ref, wvt_ref, wqit_ref, wki_ref, wwit_ref,
                   c128_ref, s128_ref, c64_ref, s64_ref, ct64_ref, st64_ref, ct32_ref, st32_ref,
                   qt_ref, k_ref, kb_ref, v_ref, vt_ref, qit_ref, ki_ref, kib_ref, wit_ref):
    h = _rms(x_ref[...], g_ref[...]).astype(BF16)
    qt = _dot_nt(wqt_ref[...], h)
    cos, sin = ct64_ref[...], st64_ref[...]
    half = DH_B // 2
    for hh in range(H_B):
        base = hh * DH_B
        o1, o2 = _rope_rows(qt[base:base + half, :], qt[base + half:base + DH_B, :], cos, sin)
        qt_ref[0, base:base + half, :] = (o1 * (DH_B ** -0.5 * LOG2E)).astype(BF16)
        qt_ref[0, base + half:base + DH_B, :] = (o2 * (DH_B ** -0.5 * LOG2E)).astype(BF16)
    k = _rope_slab(_dot(h, wk_ref[...]), c128_ref[...], s128_ref[...], DH_B)
    k_ref[...] = k
    kb_ref[...] = k.astype(BF16)
    v_ref[...] = _dot(h, wv_ref[...])
    vt_ref[0] = _dot_nt(wvt_ref[...], h).astype(BF16)
    qit = _dot_nt(wqit_ref[...], h)
    cos, sin = ct32_ref[...], st32_ref[...]
    half = DH_IDX // 2
    for hh in range(H_IDX):
        base = hh * DH_IDX
        o1, o2 = _rope_rows(qit[base:base + half, :], qit[base + half:base + DH_IDX, :], cos, sin)
        qit_ref[0, base:base + half, :] = (o1 * (DH_IDX ** -0.5)).astype(BF16)
        qit_ref[0, base + half:base + DH_IDX, :] = (o2 * (DH_IDX ** -0.5)).astype(BF16)
    ki = _rope_slab(_dot(h, wki_ref[...]), c64_ref[...], s64_ref[...], DH_IDX)[:, :DH_IDX]
    ki_ref[...] = ki
    kib_ref[...] = ki.astype(BF16)
    wit_ref[0] = _dot_nt(wwit_ref[...], h) * (H_IDX ** -0.5)


def _proj_b(x, g, w, row_tabs, time_tabs, b, t, tm=256):
    m, d = x.shape
    tm = min(tm, t)
    nt = t // tm
    in_specs = [pl.BlockSpec((tm, d), _row), pl.BlockSpec((1, d), _const)]
    in_specs += [pl.BlockSpec(a.shape, _const) for a in w]
    in_specs += [pl.BlockSpec((tm, LANES), _row_tab(nt)) for _ in row_tabs]
    in_specs += [pl.BlockSpec((tb.shape[0], tm), _time_tab(nt)) for tb in time_tabs]
    nq, ni = H_B * DH_B, H_IDX * DH_IDX
    out_specs = [pl.BlockSpec((1, nq, tm), _feat_time(nt)), pl.BlockSpec((tm, DH_B), _row),
                 pl.BlockSpec((tm, DH_B), _row), pl.BlockSpec((tm, DH_B), _row),
                 pl.BlockSpec((1, DH_B, tm), _feat_time(nt)), pl.BlockSpec((1, ni, tm), _feat_time(nt)),
                 pl.BlockSpec((tm, DH_IDX), _row), pl.BlockSpec((tm, DH_IDX), _row),
                 pl.BlockSpec((1, H_IDX, tm), _feat_time(nt))]
    out_shape = [jax.ShapeDtypeStruct((b, nq, t), BF16), jax.ShapeDtypeStruct((m, DH_B), F32),
                 jax.ShapeDtypeStruct((m, DH_B), BF16), jax.ShapeDtypeStruct((m, DH_B), F32),
                 jax.ShapeDtypeStruct((b, DH_B, t), BF16), jax.ShapeDtypeStruct((b, ni, t), BF16),
                 jax.ShapeDtypeStruct((m, DH_IDX), F32), jax.ShapeDtypeStruct((m, DH_IDX), BF16),
                 jax.ShapeDtypeStruct((b, H_IDX, t), F32)]
    return pl.pallas_call(
        _proj_b_kernel,
        grid=(m // tm,),
        in_specs=in_specs,
        out_specs=out_specs,
        out_shape=out_shape,
        compiler_params=_cparams(("parallel",)),
        name="proj_b",
    )(x, g.reshape(1, d), *w, *row_tabs, *time_tabs)


def _dsa_kernel(qt_ref, qit_ref, wit_ref, k_ref, vt_ref, ki_ref, o_ref, *, tq, lk, l_valid, q_off, n_sel):
    i = pl.program_id(1)
    kpos = lax.broadcasted_iota(jnp.int32, (lk, tq), 0)
    qpos = q_off + i * tq + lax.broadcasted_iota(jnp.int32, (lk, tq), 1)
    valid = ((kpos >> CHUNK_SHIFT) <= (qpos >> CHUNK_SHIFT)) & (kpos < l_valid)

    ki = ki_ref[0]
    score = jnp.zeros((lk, tq), F32)
    for h in range(H_IDX):
        d = _dot(ki, qit_ref[0, h * DH_IDX:(h + 1) * DH_IDX, :])
        score = score + wit_ref[0, h:h + 1, :] * jnp.maximum(d, 0.0)
    score = jnp.where(valid, score + 0.0, -jnp.inf)

    bits = pltpu.bitcast(score, jnp.int32)
    key = jnp.where(bits < 0, bits ^ jnp.int32(0x7FFFFFFF), bits)

    int_min = jnp.int32(-2 ** 31)
    nsel_f = jnp.float32(n_sel)

    def count(pred):
        return jnp.sum(jnp.where(pred, 1.0, 0.0), axis=0, keepdims=True)

    def thr_body(it, t):
        cand = t + lax.shift_left(jnp.int32(1), jnp.int32(31) - it)
        return jnp.where(count(key >= cand) >= nsel_f, cand, t)

    thr = lax.fori_loop(0, 32, thr_body, jnp.full((1, tq), int_min, jnp.int32))

    nbits = int(lk).bit_length()

    def tie_bound():
        need = nsel_f - count(key > thr)

        def tie_body(it, bound):
            cand = bound + lax.shift_left(jnp.int32(1), jnp.int32(nbits - 1) - it)
            return jnp.where(count((key == thr) & (kpos < cand)) <= need, cand, bound)

        return lax.fori_loop(0, nbits, tie_body, jnp.zeros((1, tq), jnp.int32))

    excess = jnp.max(count(key >= thr)) > nsel_f
    bound = lax.cond(excess, tie_bound, lambda: jnp.full((1, tq), 1 << nbits, jnp.int32))

    sel = ((key > thr) | ((key == thr) & (kpos < bound))) & valid
    selbias = jnp.where(sel, 0.0, NEG)

    k = k_ref[0]
    vt = vt_ref[0]

    def qk(h):
        return _dot(k, qt_ref[0, h * DH_B:(h + 1) * DH_B, :])

    s_next = qk(0)
    for h in range(H_B):
        s = s_next + selbias
        if h + 1 < H_B:
            s_next = qk(h + 1)
        m = jnp.max(s, axis=0, keepdims=True)
        p = jnp.exp2(s - m)
        l = jnp.sum(p, axis=0, keepdims=True)
        ot = _dot(vt, p.astype(BF16)) / l
        o_ref[0, :, h * DH_B:(h + 1) * DH_B] = ot.T.astype(BF16)


def _dsa_attention(qt, qit, wit, k, vt, ki, *, tq, l_valid, q_off, n_sel, groups):
    b, _, t = qt.shape
    lpad = k.shape[1]
    tg = t // groups
    outs = []
    for gi in range(groups):
        q_end = q_off + (gi + 1) * tg - 1
        k_end = min(l_valid, ((q_end >> CHUNK_SHIFT) + 1) << CHUNK_SHIFT)
        lk = min(lpad, _round_up(k_end, LANES))
        nqg = tg // tq
        qmap = lambda bi, i, gi=gi, nqg=nqg: (bi, 0, gi * nqg + i)
        kmap = lambda bi, i: (bi, 0, 0)
        outs.append(pl.pallas_call(
            functools.partial(_dsa_kernel, tq=tq, lk=lk, l_valid=l_valid, q_off=q_off + gi * tg, n_sel=n_sel),
            grid=(b, nqg),
            in_specs=[
                pl.BlockSpec((1, qt.shape[1], tq), qmap),
                pl.BlockSpec((1, qit.shape[1], tq), qmap),
                pl.BlockSpec((1, wit.shape[1], tq), qmap),
                pl.BlockSpec((1, lk, DH_B), kmap),
                pl.BlockSpec((1, DH_B, lk), kmap),
                pl.BlockSpec((1, lk, DH_IDX), kmap),
            ],
            out_specs=pl.BlockSpec((1, tq, qt.shape[1]), lambda bi, i: (bi, i, 0)),
            out_shape=jax.ShapeDtypeStruct((b, tg, qt.shape[1]), BF16),
            compiler_params=_cparams(("parallel", "parallel")),
            name="dsa_attention",
        )(qt, qit, wit, k, vt, ki))
    return outs[0] if groups == 1 else jnp.concatenate(outs, axis=1)


def _proj_c_kernel(x_ref, g_ref, wcq_ref, wckv_ref, wkpe_ref, gq_ref, gkv_ref, wuqt_ref,
                   ck_ref, sk_ref, ct_ref, st_ref, qt_ref, ckv_ref, kpe_ref):
    h = _rms(x_ref[...], g_ref[...]).astype(BF16)
    cq = _rms(_dot(h, wcq_ref[...]), gq_ref[...]).astype(BF16)
    ckv_ref[...] = _rms(_dot(h, wckv_ref[...]), gkv_ref[...])
    kpe_ref[...] = _rope_slab(_dot(h, wkpe_ref[...]), ck_ref[...], sk_ref[...], QK_ROPE)
    qt = _dot_nt(wuqt_ref[...], cq)
    cos, sin = ct_ref[...], st_ref[...]
    scale = (QK_NOPE + QK_ROPE) ** -0.5 * LOG2E
    half = QK_ROPE // 2
    for hh in range(H_C):
        base = hh * LANES
        r0 = base + QK_NOPE
        o1, o2 = _rope_rows(qt[r0:r0 + half, :], qt[r0 + half:r0 + QK_ROPE, :], cos, sin)
        qt_ref[0, base:r0, :] = (qt[base:r0, :] * scale).astype(BF16)
        qt_ref[0, r0:r0 + QK_ROPE, :] = (jnp.concatenate([o1, o2], axis=0) * scale).astype(BF16)
        qt_ref[0, r0 + QK_ROPE:base + LANES, :] = jnp.zeros((LANES - QK_NOPE - QK_ROPE, qt.shape[1]), BF16)


def _proj_c(x, g, w, gq, gkv, wuqt, row_tabs, time_tabs, b, t, tm=256):
    m, d = x.shape
    tm = min(tm, t)
    nt = t // tm
    in_specs = [pl.BlockSpec((tm, d), _row), pl.BlockSpec((1, d), _const)]
    in_specs += [pl.BlockSpec(a.shape, _const) for a in w]
    in_specs += [pl.BlockSpec((1, Q_LORA), _const), pl.BlockSpec((1, KV_LORA), _const),
                 pl.BlockSpec(wuqt.shape, _const)]
    in_specs += [pl.BlockSpec((tm, LANES), _row_tab(nt)) for _ in row_tabs]
    in_specs += [pl.BlockSpec((tb.shape[0], tm), _time_tab(nt)) for tb in time_tabs]
    nq = H_C * LANES
    return pl.pallas_call(
        _proj_c_kernel,
        grid=(m // tm,),
        in_specs=in_specs,
        out_specs=[pl.BlockSpec((1, nq, tm), _feat_time(nt)), pl.BlockSpec((tm, KV_LORA), _row),
                   pl.BlockSpec((tm, LANES), _row)],
        out_shape=[jax.ShapeDtypeStruct((b, nq, t), BF16), jax.ShapeDtypeStruct((m, KV_LORA), F32),
                   jax.ShapeDtypeStruct((m, LANES), F32)],
        compiler_params=_cparams(("parallel",)),
        name="proj_c",
    )(x, g.reshape(1, d), *w, gq.reshape(1, Q_LORA), gkv.reshape(1, KV_LORA), wuqt, *row_tabs, *time_tabs)


def _kv_expand_kernel(ckv_ref, kpe_ref, wuk_ref, wuvt_ref, place_ref, k_ref, vt_ref):
    ckv = ckv_ref[0].astype(BF16)
    kpe = _dot(kpe_ref[0].astype(BF16), place_ref[...])
    kn = _dot(ckv, wuk_ref[...])
    for hh in range(H_C):
        sl = slice(hh * LANES, (hh + 1) * LANES)
        k_ref[0, :, sl] = (kn[:, sl] + kpe).astype(BF16)
    _store_aug_values(vt_ref, _dot_nt(wuvt_ref[...], ckv).astype(BF16))


def _kv_expand(ckv, kpe, wuk, wuvt, place, tm=256):
    b, l, _ = ckv.shape
    tm = min(tm, l)
    if l % tm:
        tm = LANES
    rmap = lambda bi, i: (bi, i, 0)
    cmap = lambda bi, i: (0, 0)
    return pl.pallas_call(
        _kv_expand_kernel,
        grid=(b, l // tm),
        in_specs=[pl.BlockSpec((1, tm, KV_LORA), rmap), pl.BlockSpec((1, tm, LANES), rmap),
                  pl.BlockSpec(wuk.shape, cmap), pl.BlockSpec(wuvt.shape, cmap),
                  pl.BlockSpec(place.shape, cmap)],
        out_specs=[pl.BlockSpec((1, tm, H_C * LANES), rmap),
                   pl.BlockSpec((1, H_C * VROWS, tm), lambda bi, i: (bi, 0, i))],
        out_shape=[jax.ShapeDtypeStruct((b, l, H_C * LANES), BF16), jax.ShapeDtypeStruct((b, H_C * VROWS, l), BF16)],
        compiler_params=_cparams(("parallel", "parallel")),
        name="kv_expand",
    )(ckv, kpe, wuk, wuvt, place)


def _rope_angles(pos, dim):
    half = dim // 2
    inv = ROPE_THETA ** (-jnp.arange(half, dtype=F32) / half)
    ang = pos.astype(F32)[:, None] * inv[None, :]
    return jnp.cos(ang), jnp.sin(ang)


def _rope_row_tables(pos, dim):
    cos, sin = _rope_angles(pos, dim)
    reps = LANES // dim
    return (jnp.tile(jnp.concatenate([cos, cos], axis=-1), (1, reps)),
            jnp.tile(jnp.concatenate([-sin, sin], axis=-1), (1, reps)))


def _rope_time_tables(pos, dim):
    cos, sin = _rope_angles(pos, dim)
    return cos.T, sin.T


def _prep_weights(p):
    w = {}
    w["ffn1_in"] = p["ffn1_w_in"].astype(BF16)
    w["ffn1_out"] = p["ffn1_w_out"].astype(BF16)
    w["ffn2_in"] = p["ffn2_w_in"].astype(BF16)
    w["ffn2_out"] = p["ffn2_w_out"].astype(BF16)
    a = p["a_w_in"].astype(BF16)
    n = H_A * DH_A
    w["a"] = (a[:, :n].T, a[:, n:2 * n], a[:, 2 * n:3 * n], a[:, 2 * n:3 * n].T, a[:, 3 * n:])
    w["a_out"] = p["a_w_out"].astype(BF16)
    bw = p["b_w_in"].astype(BF16)
    offs = np.cumsum([0, H_B * DH_B, DH_B, DH_B, H_IDX * DH_IDX, DH_IDX, H_IDX]).tolist()
    bq, bk, bv, bqi, bki, bwi = [bw[:, offs[i]:offs[i + 1]] for i in range(6)]
    w["b"] = (bq.T, bk, bv, bv.T, bqi.T, jnp.concatenate([bki, bki], axis=1), bwi.T)
    w["b_out"] = p["b_w_out"].astype(BF16)
    c = p["c_w_down"].astype(BF16)
    d = c.shape[0]
    wkpe = jnp.zeros((d, LANES), BF16).at[:, :QK_ROPE].set(c[:, Q_LORA + KV_LORA:])
    w["c"] = (c[:, :Q_LORA], c[:, Q_LORA:Q_LORA + KV_LORA], wkpe)
    uq = p["c_w_uq"].astype(BF16).reshape(Q_LORA, H_C, QK_NOPE + QK_ROPE)
    uq = jnp.pad(uq, ((0, 0), (0, 0), (0, LANES - QK_NOPE - QK_ROPE)))
    w["c_uqt"] = uq.reshape(Q_LORA, H_C * LANES).T
    ukv = p["c_w_ukv"].astype(BF16).reshape(KV_LORA, H_C, QK_NOPE + V_C)
    uk = jnp.pad(ukv[:, :, :QK_NOPE], ((0, 0), (0, 0), (0, LANES - QK_NOPE)))
    w["c_uk"] = uk.reshape(KV_LORA, H_C * LANES)
    w["c_uvt"] = ukv[:, :, QK_NOPE:].reshape(KV_LORA, H_C * V_C).T
    w["c_place"] = jnp.zeros((LANES, LANES), BF16).at[jnp.arange(QK_ROPE), QK_NOPE + jnp.arange(QK_ROPE)].set(1.0)
    w["c_out"] = p["c_w_out"].astype(BF16)
    dw = p["d_w_in"].astype(BF16)
    n = H_D * DH_D
    w["d"] = (dw[:, :n].T, dw[:, n:2 * n], dw[:, 2 * n:], dw[:, 2 * n:].T)
    w["d_out"] = p["d_w_out"].astype(BF16)
    return w


def _round_up(x, m):
    return (x + m - 1) // m * m


def _cat_time(past, new, lpad, axis):
    a = new if past is None else jnp.concatenate([past.astype(new.dtype), new], axis=axis)
    if a.shape[axis] < lpad:
        pad = [(0, 0)] * a.ndim
        pad[axis] = (0, lpad - a.shape[axis])
        a = jnp.pad(a, pad)
    return a


def _trunk(x3, past, p, w):
    b, t, d = x3.shape
    m = b * t
    x = x3.reshape(m, d)
    past_len = 0 if past is None else past[0][0].shape[1]
    l_valid = past_len + t
    pos = past_len + jnp.arange(t, dtype=jnp.int32)
    if past is None:
        tile = min(256, t)
        tile_s = min(512, t)
        lpad = _round_up(l_valid, tile_s)
        ktile = tile
        tpad = t
        groups = 4 if t % (4 * tile) == 0 else 1
    else:
        tile = tile_s = tpad = _round_up(t, LANES)
        lpad = ktile = _round_up(l_valid, LANES)
        groups = 1
    ktile_s = tile_s if past is None else lpad
    norm_g = p["norm_g"]
    rows = {}

    def r3(a):
        return a.reshape(b, t, a.shape[-1])

    def qpad(a):
        return a if tpad == t else jnp.pad(a, ((0, 0), (0, 0), (0, tpad - t)))

    def unpad(o):
        return (o if tpad == t else o[:, :t]).reshape(m, -1)

    def pst(idx, j, width):
        return None if past is None else past[idx][j].reshape(b, past_len, width)

    def pst_t(idx, j, width):
        return None if past is None else jnp.transpose(past[idx][j].reshape(b, past_len, width), (0, 2, 1))

    def aug(vt_past):
        if vt_past is None:
            return None
        dv = LANES // 2
        v4 = vt_past.reshape(b, -1, dv, past_len)
        extra = jnp.zeros((b, v4.shape[1], VROWS - dv, past_len), v4.dtype).at[:, :, 0, :].set(1.0)
        return jnp.concatenate([v4, extra], axis=2).reshape(b, -1, past_len)

    for i in range(4):
        x = _ffn(x, norm_g[i, 0], w["ffn1_in"][i], w["ffn1_out"][i])
        g = norm_g[i, 1]
        if i == 0:
            wqt, wk, wv, wvt, wf = w["a"]
            qt, k, kb, v, vt, lf = _proj_qkv(x, g, wqt, wk, wv, wvt, DH_A ** -0.5 * LOG2E, b, t, wf, p["a_b_f"])
            rows["a"] = (k.reshape(b, t, H_A, DH_A), v.reshape(b, t, H_A, DH_A), lf.reshape(b, t, H_A))
            c = _cumsum_time(_cat_time(None if past is None else past[0][2].astype(F32), r3(lf), lpad, 1))
            o = _attention(qpad(qt), _cat_time(pst(0, 0, H_A * DH_A), r3(kb), lpad, 1),
                           _cat_time(aug(pst_t(0, 1, H_A * DH_A)), vt, lpad, 2), c,
                           nh=H_A, k_per_head=False, chunked=False, tq=tile_s, tk=ktile_s, q_off=past_len,
                           l_valid=l_valid)
            x = _outproj(x, unpad(o), w["a_out"])
        elif i == 1:
            row_tabs = list(_rope_row_tables(pos, DH_B)) + list(_rope_row_tables(pos, DH_IDX))
            time_tabs = list(_rope_time_tables(pos, DH_B)) + list(_rope_time_tables(pos, DH_IDX))
            qt, k, kb, v, vt, qit, ki, kib, wit = _proj_b(x, g, w["b"], row_tabs, time_tabs, b, t)
            rows["b"] = (r3(k), r3(v), r3(ki))
            n_sel = min(IDX_TOPK, l_valid // 4)
            o = _dsa_attention(qpad(qt), qpad(qit), qpad(wit), _cat_time(pst(1, 0, DH_B), r3(kb), lpad, 1),
                               _cat_time(pst_t(1, 1, DH_B), vt, lpad, 2),
                               _cat_time(pst(1, 2, DH_IDX), r3(kib), lpad, 1),
                               tq=tile, l_valid=l_valid, q_off=past_len, n_sel=n_sel, groups=groups)
            x = _outproj(x, unpad(o), w["b_out"])
        elif i == 2:
            row_tabs = list(_rope_row_tables(pos, QK_ROPE))
            time_tabs = list(_rope_time_tables(pos, QK_ROPE))
            qt, ckv, kpe = _proj_c(x, g, w["c"], p["c_g_q"], p["c_g_kv"], w["c_uqt"], row_tabs, time_tabs, b, t)
            rows["c"] = (r3(ckv), r3(kpe)[:, :, :QK_ROPE])
            kpe_past = None if past is None else jnp.pad(past[2][1].astype(F32),
                                                         ((0, 0), (0, 0), (0, LANES - QK_ROPE)))
            ckv_all = _cat_time(None if past is None else past[2][0].astype(F32), r3(ckv), lpad, 1)
            kpe_all = _cat_time(kpe_past, r3(kpe), lpad, 1)
            kc, vct = _kv_expand(ckv_all, kpe_all, w["c_uk"], w["c_uvt"], w["c_place"])
            o = _attention(qpad(qt), kc, vct, None,
                           nh=H_C, k_per_head=True, chunked=True, tq=tile_s, tk=ktile_s, q_off=past_len,
                           l_valid=l_valid)
            x = _outproj(x, unpad(o), w["c_out"])
        else:
            wqt, wk, wv, wvt = w["d"]
            qt, k, kb, v, vt = _proj_qkv(x, g, wqt, wk, wv, wvt, DH_D ** -0.5 * LOG2E, b, t)
            rows["d"] = (k.reshape(b, t, H_D, DH_D), v.reshape(b, t, H_D, DH_D))
            o = _sb_attention(qpad(qt), _cat_time(pst(3, 0, H_D * DH_D), r3(kb), lpad, 1),
                              _cat_time(pst_t(3, 1, H_D * DH_D), vt, lpad, 2),
                              nh=H_D, tq=tile, tk=ktile, q_off=past_len)
            x = _outproj(x, unpad(o), w["d_out"])
        x = _ffn(x, norm_g[i, 2], w["ffn2_in"][i], w["ffn2_out"][i], p["final_g"] if i == 3 else None)
    return x.reshape(b, t, d), rows


def kernel(x_prompt, x_sample, cache_a_k, cache_a_v, cache_a_logf, cache_b_k, cache_b_v, cache_b_kidx,
           cache_c_ckv, cache_c_kpe, cache_d_k, cache_d_v, norm_g, final_g,
           ffn1_w_in, ffn1_w_out, ffn2_w_in, ffn2_w_out, a_w_in, a_b_f, a_w_out, b_w_in, b_w_out,
           c_w_down, c_g_q, c_g_kv, c_w_uq, c_w_ukv, c_w_out, d_w_in, d_w_out):
    p = dict(norm_g=norm_g, final_g=final_g, ffn1_w_in=ffn1_w_in, ffn1_w_out=ffn1_w_out,
             ffn2_w_in=ffn2_w_in, ffn2_w_out=ffn2_w_out, a_w_in=a_w_in, a_b_f=a_b_f, a_w_out=a_w_out,
             b_w_in=b_w_in, b_w_out=b_w_out, c_w_down=c_w_down, c_g_q=c_g_q, c_g_kv=c_g_kv,
             c_w_uq=c_w_uq, c_w_ukv=c_w_ukv, c_w_out=c_w_out, d_w_in=d_w_in, d_w_out=d_w_out)
    w = _prep_weights(p)
    y_prompt, pr = _trunk(x_prompt, None, p, w)
    past = ((cache_a_k, cache_a_v, cache_a_logf), (cache_b_k, cache_b_v, cache_b_kidx),
            (cache_c_ckv, cache_c_kpe), (cache_d_k, cache_d_v))
    y_sample, sr = _trunk(x_sample, past, p, w)
    return (y_prompt, y_sample, *pr["a"], *pr["b"], *pr["c"], *pr["d"],
            *sr["a"], *sr["b"], *sr["c"], *sr["d"])
```

```python
import functools

import jax
import jax.numpy as jnp
import numpy as np
from jax import lax
from jax.experimental import pallas as pl
from jax.experimental.pallas import tpu as pltpu

F32 = jnp.float32
BF16 = jnp.bfloat16

D_MODEL = 1024
CHUNK_SHIFT = 6
ROPE_THETA = 10000.0
RMS_EPS = 1e-6
D_FF = 2816
H_A, DH_A = 16, 64
H_B, DH_B = 8, 128
H_IDX, DH_IDX = 8, 64
IDX_TOPK = 256
H_C, Q_LORA, KV_LORA, QK_NOPE, QK_ROPE, V_C = 16, 256, 128, 64, 32, 64
H_D, DH_D = 16, 64

LANES = 128
NEG = -1e30
LOG2E = 1.4426950408889634
EXP2_ZERO = -150.0
VROWS = 80
VMEM_LIMIT = 56 * 1024 * 1024


def _cparams(sem):
    return pltpu.CompilerParams(dimension_semantics=sem, vmem_limit_bytes=VMEM_LIMIT)


def _dot(a, b):
    return jnp.dot(a, b, preferred_element_type=F32)


def _dot_nt(a, b):
    return lax.dot_general(a, b, (((1,), (1,)), ((), ())), preferred_element_type=F32)


def _rms(x, g):
    return x * lax.rsqrt(jnp.mean(x * x, axis=-1, keepdims=True) + RMS_EPS) * g


def _softplus_neg_abs(z):
    return jnp.log(1.0 + jnp.exp(-jnp.abs(z)))


def _log_sigmoid(z):
    return jnp.minimum(z, 0.0) - _softplus_neg_abs(z)


def _rope_slab(x, cos, sin, group):
    half = group // 2
    if group == LANES:
        partner = pltpu.roll(x, half, 1)
    else:
        lane = lax.broadcasted_iota(jnp.int32, (1, LANES), 1)
        first = (lane & (group - 1)) < half
        partner = jnp.where(first, pltpu.roll(x, LANES - half, 1), pltpu.roll(x, half, 1))
    return x * cos + partner * sin


def _store_aug_values(vt_ref, vt):
    dv = LANES // 2
    n = vt.shape[1]
    extra = jnp.where(lax.broadcasted_iota(jnp.int32, (VROWS - dv, n), 0) == 0, 1.0, 0.0).astype(BF16)
    for hh in range(vt.shape[0] // dv):
        vt_ref[0, hh * VROWS:hh * VROWS + dv, :] = vt[hh * dv:(hh + 1) * dv, :]
        vt_ref[0, hh * VROWS + dv:(hh + 1) * VROWS, :] = extra


def _rope_rows(x1, x2, cos, sin):
    return x1 * cos - x2 * sin, x1 * sin + x2 * cos


FFN_CHUNK = 256


def _ffn_kernel(x_ref, *rest, mix, final):
    rest = list(rest)
    if mix:
        o_ref, wmix_ref = rest.pop(0), rest.pop(0)
    g_ref, win_ref, wo_ref = rest.pop(0), rest.pop(0), rest.pop(0)
    if final:
        fg_ref = rest.pop(0)
    (y_ref,) = rest
    dff = wo_ref.shape[0]
    x = x_ref[...]
    if mix:
        x = x + _dot(o_ref[...], wmix_ref[...])
    h = _rms(x, g_ref[...]).astype(BF16)

    def inproj(c):
        lo = c * FFN_CHUNK
        return _dot(h, win_ref[:, lo:lo + FFN_CHUNK]), _dot(h, win_ref[:, dff + lo:dff + lo + FFN_CHUNK])

    acc = None
    nxt = inproj(0)
    for c in range(dff // FFN_CHUNK):
        up, gate = nxt
        if (c + 1) * FFN_CHUNK < dff:
            nxt = inproj(c + 1)
        act = (gate / (1.0 + jnp.exp(-gate)) * up).astype(BF16)
        part = _dot(act, wo_ref[c * FFN_CHUNK:(c + 1) * FFN_CHUNK, :])
        acc = part if acc is None else acc + part
    y = x + 0.5 * acc
    if final:
        y = _rms(y, fg_ref[...])
    y_ref[...] = y


def _resident(shape):
    return pl.BlockSpec(shape, lambda i: (0,) * len(shape), pipeline_mode=pl.Buffered(1))


def _ffn(x, g, w_in, w_out, mix=None, final_g=None, tm=512):
    m, d = x.shape
    dff = w_out.shape[0]
    tm = min(tm, m)
    row = pl.BlockSpec((tm, d), lambda i: (i, 0))
    in_specs, args = [row], [x]
    if mix is not None:
        o, wmix = mix
        in_specs += [pl.BlockSpec((tm, o.shape[1]), lambda i: (i, 0)), _resident(wmix.shape)]
        args += [o, wmix]
    in_specs += [_resident((1, d)), _resident((d, 2 * dff)), _resident((dff, d))]
    args += [g.reshape(1, d), w_in, w_out]
    if final_g is not None:
        in_specs.append(_resident((1, d)))
        args.append(final_g.reshape(1, d))
    return pl.pallas_call(
        functools.partial(_ffn_kernel, mix=mix is not None, final=final_g is not None),
        grid=(m // tm,),
        in_specs=in_specs,
        out_specs=row,
        out_shape=jax.ShapeDtypeStruct((m, d), F32),
        compiler_params=_cparams(("parallel",)),
        name="ffn",
    )(*args)


def _row(i):
    return (i, 0)


def _feat_time(nt):
    return lambda i: (i // nt, 0, i % nt)


def _tiled_spec(feat, tile, index_map):
    return pl.BlockSpec((None, 1, feat, tile), index_map)


def _tiled_out(nt):
    return lambda i: (i // nt, i % nt, 0, 0)


def _time_tab(nt):
    return lambda i: (0, i % nt)


def _row_tab(nt):
    return lambda i: (i % nt, 0)


def _proj_qkv_kernel(x_ref, g_ref, wqt_ref, wk_ref, wv_ref, wvt_ref, *rest, scale, has_f):
    if has_f:
        wf_ref, bf_ref, qt_ref, k_ref, kb_ref, v_ref, vt_ref, lf_ref = rest
    else:
        qt_ref, k_ref, kb_ref, v_ref, vt_ref = rest
    h = _rms(x_ref[...], g_ref[...]).astype(BF16)
    qt = (_dot_nt(wqt_ref[...], h) * scale).astype(BF16)
    half = LANES // 2
    zeros = jnp.zeros((half, qt.shape[1]), BF16)
    for hh in range(qt.shape[0] // half):
        base = hh * LANES
        own = slice(base + (hh % 2) * half, base + (hh % 2) * half + half)
        other = slice(base + (1 - hh % 2) * half, base + (1 - hh % 2) * half + half)
        qt_ref[0, own, :] = qt[hh * half:(hh + 1) * half, :]
        qt_ref[0, other, :] = zeros
    k = _dot(h, wk_ref[...])
    k_ref[...] = k
    kb_ref[...] = k.astype(BF16)
    v_ref[...] = _dot(h, wv_ref[...])
    vt = _dot_nt(wvt_ref[...], h).astype(BF16)
    if has_f:
        _store_aug_values(vt_ref, vt)
        lf_ref[...] = _log_sigmoid(_dot(h, wf_ref[...]) + bf_ref[...])
    else:
        vt_ref[0] = vt


def _proj_qkv(x, g, wqt, wk, wv, wvt, scale, b, t, wf=None, bf=None, tm=256):
    m, d = x.shape
    n = wk.shape[1]
    tm = min(tm, t)
    nt = t // tm
    has_f = wf is not None
    nvt = n // (LANES // 2) * VROWS if has_f else n
    in_specs = [pl.BlockSpec((tm, d), _row), _resident((1, d))] + [_resident((d, n))] * 4
    args = [x, g.reshape(1, d), wqt, wk, wv, wvt]
    out_specs = [_tiled_spec(2 * n, tm, _tiled_out(nt)), pl.BlockSpec((tm, n), _row),
                 pl.BlockSpec((tm, n), _row), pl.BlockSpec((tm, n), _row), _tiled_spec(nvt, tm, _tiled_out(nt))]
    out_shape = [jax.ShapeDtypeStruct((b, nt, 2 * n, tm), BF16), jax.ShapeDtypeStruct((m, n), F32),
                 jax.ShapeDtypeStruct((m, n), BF16), jax.ShapeDtypeStruct((m, n), F32),
                 jax.ShapeDtypeStruct((b, nt, nvt, tm), BF16)]
    if has_f:
        nh = wf.shape[1]
        in_specs += [_resident((d, nh)), _resident((1, nh))]
        args += [wf, bf.reshape(1, nh)]
        out_specs.append(pl.BlockSpec((tm, nh), _row))
        out_shape.append(jax.ShapeDtypeStruct((m, nh), F32))
    return pl.pallas_call(
        functools.partial(_proj_qkv_kernel, scale=scale, has_f=has_f),
        grid=(m // tm,),
        in_specs=in_specs,
        out_specs=out_specs,
        out_shape=out_shape,
        compiler_params=_cparams(("parallel",)),
        name="proj_qkv",
    )(*args)


def _split3(x):
    hi = x.astype(BF16)
    r1 = x - hi.astype(F32)
    mid = r1.astype(BF16)
    lo = (r1 - mid.astype(F32)).astype(BF16)
    return hi, mid, lo


def _cumsum_kernel(x_ref, c_ref, *, nblk):
    r = lax.broadcasted_iota(jnp.int32, (LANES, LANES), 0)
    c = lax.broadcasted_iota(jnp.int32, (LANES, LANES), 1)
    tri = jnp.where(c <= r, 1.0, 0.0).astype(BF16)
    carry = jnp.zeros((1, x_ref.shape[2]), F32)
    for b in range(nblk):
        hi, mid, lo = _split3(x_ref[0, b * LANES:(b + 1) * LANES, :])
        blk = (_dot(tri, hi) + _dot(tri, mid)) + _dot(tri, lo) + carry
        c_ref[0, b * LANES:(b + 1) * LANES, :] = blk
        carry = blk[LANES - 1:LANES, :]


def _cumsum_time(x):
    b, l, nh = x.shape
    spec = pl.BlockSpec((1, l, nh), lambda i: (i, 0, 0))
    return pl.pallas_call(
        functools.partial(_cumsum_kernel, nblk=l // LANES),
        grid=(b,),
        in_specs=[spec],
        out_specs=spec,
        out_shape=jax.ShapeDtypeStruct((b, l, nh), F32),
        compiler_params=_cparams(("parallel",)),
        name="cumsum_time",
    )(x)


def _last_block(i, tq, tk, q_off, chunked, nk):
    q_last = q_off + (i + 1) * tq - 1
    if chunked:
        q_last = ((q_last >> CHUNK_SHIFT) << CHUNK_SHIFT) + (1 << CHUNK_SHIFT) - 1
    return jnp.minimum(q_last // tk, nk - 1)


def _attn_kernel(qt_ref, k_ref, vt_ref, *rest, nh, k_per_head, bias, chunked, tq, tk, q_off, l_valid, nk):
    if bias:
        c_ref, o_ref, m_ref, acc_ref = rest
    else:
        o_ref, m_ref, acc_ref = rest
    i = pl.program_id(1)
    j = pl.program_id(2)
    dv = LANES // 2

    @pl.when(j == 0)
    def _():
        m_ref[...] = jnp.full_like(m_ref, NEG)
        acc_ref[...] = jnp.zeros_like(acc_ref)

    q_first = q_off + i * tq
    k_last = j * tk + tk - 1
    if chunked:
        full = ((k_last >> CHUNK_SHIFT) <= (q_first >> CHUNK_SHIFT)) & (k_last < l_valid)
    else:
        full = k_last <= q_first
    active = j <= _last_block(i, tq, tk, q_off, chunked, nk)

    def step(masked):
        if masked:
            kpos = j * tk + lax.broadcasted_iota(jnp.int32, (tk, tq), 0)
            qpos = q_first + lax.broadcasted_iota(jnp.int32, (tk, tq), 1)
            if chunked:
                valid = ((kpos >> CHUNK_SHIFT) <= (qpos >> CHUNK_SHIFT)) & (kpos < l_valid)
            else:
                valid = kpos <= qpos
            maskbias = jnp.where(valid, 0.0, NEG)
        def qk(h):
            ks = h if k_per_head else h // 2
            return _dot(k_ref[0, :, ks * LANES:(ks + 1) * LANES], qt_ref[0, h * LANES:(h + 1) * LANES, :])

        s_next = qk(0)
        for h in range(nh):
            s = s_next
            if h + 1 < nh:
                s_next = qk(h + 1)
            if bias:
                s = s - c_ref[0, :, h:h + 1] * LOG2E
            if masked:
                s = s + maskbias
            m_prev = m_ref[h:h + 1, :]
            m_new = jnp.maximum(m_prev, jnp.max(s, axis=0, keepdims=True))
            alpha = jnp.exp2(m_prev - m_new)
            p = jnp.exp2(s - m_new)
            m_ref[h:h + 1, :] = m_new
            rows = slice(h * VROWS, (h + 1) * VROWS)
            acc_ref[rows, :] = alpha * acc_ref[rows, :] + _dot(vt_ref[0, rows, :], p.astype(BF16))

    @pl.when(active & full)
    def _():
        step(False)

    @pl.when(active & jnp.logical_not(full))
    def _():
        step(True)

    @pl.when(j == nk - 1)
    def _():
        for p in range(nh // 2):
            parts = [acc_ref[h * VROWS:h * VROWS + dv, :] / acc_ref[h * VROWS + dv:h * VROWS + dv + 1, :]
                     for h in (2 * p, 2 * p + 1)]
            o_ref[0, :, p * LANES:(p + 1) * LANES] = jnp.concatenate(parts, axis=0).T.astype(BF16)


def _attention(qt, k, vt, c, *, nh, k_per_head, chunked, q_off, l_valid):
    b, nq, _, tq = qt.shape
    nk, tk = vt.shape[1], vt.shape[3]
    assert k.shape[1] == nk * tk
    bias = c is not None

    def kblk(i, j):
        return jnp.minimum(j, _last_block(i, tq, tk, q_off, chunked, nk))

    in_specs = [
        _tiled_spec(qt.shape[2], tq, lambda bi, i, j: (bi, i, 0, 0)),
        pl.BlockSpec((1, tk, k.shape[2]), lambda bi, i, j: (bi, kblk(i, j), 0)),
        _tiled_spec(vt.shape[2], tk, lambda bi, i, j: (bi, kblk(i, j), 0, 0)),
    ]
    args = [qt, k, vt]
    if bias:
        in_specs.append(pl.BlockSpec((1, tk, c.shape[2]), lambda bi, i, j: (bi, kblk(i, j), 0)))
        args.append(c)
    dv = nh * (LANES // 2)
    return pl.pallas_call(
        functools.partial(_attn_kernel, nh=nh, k_per_head=k_per_head, bias=bias, chunked=chunked,
                          tq=tq, tk=tk, q_off=q_off, l_valid=l_valid, nk=nk),
        grid=(b, nq, nk),
        in_specs=in_specs,
        out_specs=pl.BlockSpec((1, tq, dv), lambda bi, i, j: (bi, i, 0)),
        out_shape=jax.ShapeDtypeStruct((b, nq * tq, dv), BF16),
        scratch_shapes=[pltpu.VMEM((nh, tq), F32), pltpu.VMEM((vt.shape[2], tq), F32)],
        compiler_params=_cparams(("parallel", "parallel", "arbitrary")),
        name="attention",
    )(*args)


def _sb_kernel(qt_ref, k_ref, vt_ref, o_ref, tail_ref, acc_ref, *, nh, tq, tk, q_off, nk):
    i = pl.program_id(1)
    j = pl.program_id(2)
    dv = LANES // 2

    @pl.when(j == 0)
    def _():
        tail_ref[...] = jnp.zeros_like(tail_ref)
        acc_ref[...] = jnp.zeros_like(acc_ref)

    last = _last_block(i, tq, tk, q_off, False, nk)
    jj = last - j
    q_first = q_off + i * tq
    full = jj * tk + tk - 1 < q_first
    active = (j <= last) & (jnp.max(tail_ref[...]) > EXP2_ZERO)

    def step(masked):
        r = lax.broadcasted_iota(jnp.int32, (tk + 16, tk), 0)
        c = lax.broadcasted_iota(jnp.int32, (tk + 16, tk), 1)
        after = jnp.where(((r < tk) & (c > r)) | (r == tk), 1.0, 0.0).astype(BF16)
        if masked:
            kpos = jj * tk + lax.broadcasted_iota(jnp.int32, (tk, tq), 0)
            qpos = q_first + lax.broadcasted_iota(jnp.int32, (tk, tq), 1)
            valid = kpos < qpos

        def qk(h):
            return _dot(k_ref[0, :, (h // 2) * LANES:(h // 2 + 1) * LANES], qt_ref[0, h * LANES:(h + 1) * LANES, :])

        def logs(z):
            log_beta = jnp.minimum(z, 0.0) - jnp.log2(1.0 + jnp.exp2(-jnp.abs(z)))
            log_1mb = log_beta - z
            if masked:
                log_1mb = jnp.where(valid, log_1mb, 0.0)
            return log_beta, _dot(after, log_1mb.astype(BF16))

        def finish(h, log_beta, sums):
            t_prev = tail_ref[h:h + 1, :]
            a = jnp.exp2(log_beta + sums[:tk, :] + t_prev)
            if masked:
                a = jnp.where(valid, a, 0.0)
            tail_ref[h:h + 1, :] = t_prev + sums[tk:tk + 1, :]
            rows = slice(h * dv, (h + 1) * dv)
            acc_ref[rows, :] += _dot(vt_ref[0, rows, :], a.astype(BF16))

        z_next = qk(0)
        pending = None
        for h in range(nh):
            z = z_next
            if h + 1 < nh:
                z_next = qk(h + 1)
            cur = logs(z)
            if pending is not None:
                finish(h - 1, *pending)
            pending = cur
        finish(nh - 1, *pending)

    @pl.when(active & full)
    def _():
        step(False)

    @pl.when(active & jnp.logical_not(full))
    def _():
        step(True)

    @pl.when(j == nk - 1)
    def _():
        for p in range(nh // 2):
            o_ref[0, :, p * LANES:(p + 1) * LANES] = acc_ref[p * LANES:(p + 1) * LANES, :].T.astype(BF16)


def _sb_attention(qt, k, vt, *, nh, q_off):
    b, nq, _, tq = qt.shape
    nk, dv, tk = vt.shape[1:]
    assert k.shape[1] == nk * tk

    def kblk(i, j):
        return jnp.maximum(_last_block(i, tq, tk, q_off, False, nk) - j, 0)

    return pl.pallas_call(
        functools.partial(_sb_kernel, nh=nh, tq=tq, tk=tk, q_off=q_off, nk=nk),
        grid=(b, nq, nk),
        in_specs=[
            _tiled_spec(qt.shape[2], tq, lambda bi, i, j: (bi, i, 0, 0)),
            pl.BlockSpec((1, tk, k.shape[2]), lambda bi, i, j: (bi, kblk(i, j), 0)),
            _tiled_spec(dv, tk, lambda bi, i, j: (bi, kblk(i, j), 0, 0)),
        ],
        out_specs=pl.BlockSpec((1, tq, dv), lambda bi, i, j: (bi, i, 0)),
        out_shape=jax.ShapeDtypeStruct((b, nq * tq, dv), BF16),
        scratch_shapes=[pltpu.VMEM((nh, tq), F32), pltpu.VMEM((dv, tq), F32)],
        compiler_params=_cparams(("parallel", "parallel", "arbitrary")),
        name="sb_attention",
    )(qt, k, vt)


def _proj_b_kernel(x_ref, g_ref, wqt_ref, wk_ref, wv_ref, wvt_ref, wqit_ref, wki_ref, wwit_ref,
                   c128_ref, s128_ref, c64_ref, s64_ref, ct64_ref, st64_ref, ct32_ref, st32_ref,
                   qt_ref, k_ref, kb_ref, v_ref, vt_ref, qit_ref, ki_ref, kib_ref, wit_ref):
    h = _rms(x_ref[...], g_ref[...]).astype(BF16)
    qt = _dot_nt(wqt_ref[...], h)
    cos, sin = ct64_ref[...], st64_ref[...]
    half = DH_B // 2
    for hh in range(H_B):
        base = hh * DH_B
        o1, o2 = _rope_rows(qt[base:base + half, :], qt[base + half:base + DH_B, :], cos, sin)
        qt_ref[0, base:base + half, :] = (o1 * (DH_B ** -0.5 * LOG2E)).astype(BF16)
        qt_ref[0, base + half:base + DH_B, :] = (o2 * (DH_B ** -0.5 * LOG2E)).astype(BF16)
    k = _rope_slab(_dot(h, wk_ref[...]), c128_ref[...], s128_ref[...], DH_B)
    k_ref[...] = k
    kb_ref[...] = k.astype(BF16)
    v_ref[...] = _dot(h, wv_ref[...])
    vt_ref[0] = _dot_nt(wvt_ref[...], h).astype(BF16)
    qit = _dot_nt(wqit_ref[...], h)
    cos, sin = ct32_ref[...], st32_ref[...]
    half = DH_IDX // 2
    for hh in range(H_IDX):
        base = hh * DH_IDX
        o1, o2 = _rope_rows(qit[base:base + half, :], qit[base + half:base + DH_IDX, :], cos, sin)
        qit_ref[0, base:base + half, :] = (o1 * (DH_IDX ** -0.5)).astype(BF16)
        qit_ref[0, base + half:base + DH_IDX, :] = (o2 * (DH_IDX ** -0.5)).astype(BF16)
    ki = _rope_slab(_dot(h, wki_ref[...]), c64_ref[...], s64_ref[...], DH_IDX)[:, :DH_IDX]
    ki_ref[...] = ki
    kib_ref[...] = ki.astype(BF16)
    wit_ref[0] = _dot_nt(wwit_ref[...], h) * (H_IDX ** -0.5)


def _proj_b(x, g, w, row_tabs, time_tabs, b, t, tm=256):
    m, d = x.shape
    tm = min(tm, t)
    nt = t // tm
    in_specs = [pl.BlockSpec((tm, d), _row), _resident((1, d))]
    in_specs += [_resident(a.shape) for a in w]
    in_specs += [pl.BlockSpec((tm, LANES), _row_tab(nt)) for _ in row_tabs]
    in_specs += [pl.BlockSpec((tb.shape[0], tm), _time_tab(nt)) for tb in time_tabs]
    nq, ni = H_B * DH_B, H_IDX * DH_IDX
    out_specs = [_tiled_spec(nq, tm, _tiled_out(nt)), pl.BlockSpec((tm, DH_B), _row),
                 pl.BlockSpec((tm, DH_B), _row), pl.BlockSpec((tm, DH_B), _row),
                 pl.BlockSpec((1, DH_B, tm), _feat_time(nt)), _tiled_spec(ni, tm, _tiled_out(nt)),
                 pl.BlockSpec((tm, DH_IDX), _row), pl.BlockSpec((tm, DH_IDX), _row),
                 _tiled_spec(H_IDX, tm, _tiled_out(nt))]
    out_shape = [jax.ShapeDtypeStruct((b, nt, nq, tm), BF16), jax.ShapeDtypeStruct((m, DH_B), F32),
                 jax.ShapeDtypeStruct((m, DH_B), BF16), jax.ShapeDtypeStruct((m, DH_B), F32),
                 jax.ShapeDtypeStruct((b, DH_B, t), BF16), jax.ShapeDtypeStruct((b, nt, ni, tm), BF16),
                 jax.ShapeDtypeStruct((m, DH_IDX), F32), jax.ShapeDtypeStruct((m, DH_IDX), BF16),
                 jax.ShapeDtypeStruct((b, nt, H_IDX, tm), F32)]
    return pl.pallas_call(
        _proj_b_kernel,
        grid=(m // tm,),
        in_specs=in_specs,
        out_specs=out_specs,
        out_shape=out_shape,
        compiler_params=_cparams(("parallel",)),
        name="proj_b",
    )(x, g.reshape(1, d), *w, *row_tabs, *time_tabs)


def _dsa_kernel(qt_ref, qit_ref, wit_ref, k_ref, vt_ref, ki_ref, o_ref, *, tq, lk, l_valid, q_off, n_sel):
    i = pl.program_id(1)
    kpos = lax.broadcasted_iota(jnp.int32, (lk, tq), 0)
    qpos = q_off + i * tq + lax.broadcasted_iota(jnp.int32, (lk, tq), 1)
    valid = ((kpos >> CHUNK_SHIFT) <= (qpos >> CHUNK_SHIFT)) & (kpos < l_valid)

    ki = ki_ref[0]
    score = jnp.zeros((lk, tq), F32)
    for h in range(H_IDX):
        d = _dot(ki, qit_ref[0, h * DH_IDX:(h + 1) * DH_IDX, :])
        score = score + wit_ref[0, h:h + 1, :] * jnp.maximum(d, 0.0)
    score = jnp.where(valid, score + 0.0, -jnp.inf)

    bits = pltpu.bitcast(score, jnp.int32)
    key = jnp.where(bits < 0, bits ^ jnp.int32(0x7FFFFFFF), bits)

    int_min = jnp.int32(-2 ** 31)
    nsel_f = jnp.float32(n_sel)

    def count(pred):
        return jnp.sum(jnp.where(pred, 1.0, 0.0), axis=0, keepdims=True)

    def thr_body(it, t):
        cand = t + lax.shift_left(jnp.int32(1), jnp.int32(31) - it)
        return jnp.where(count(key >= cand) >= nsel_f, cand, t)

    thr = lax.fori_loop(0, 32, thr_body, jnp.full((1, tq), int_min, jnp.int32))

    nbits = int(lk).bit_length()

    def tie_bound():
        need = nsel_f - count(key > thr)

        def tie_body(it, bound):
            cand = bound + lax.shift_left(jnp.int32(1), jnp.int32(nbits - 1) - it)
            return jnp.where(count((key == thr) & (kpos < cand)) <= need, cand, bound)

        return lax.fori_loop(0, nbits, tie_body, jnp.zeros((1, tq), jnp.int32))

    excess = jnp.max(count(key >= thr)) > nsel_f
    bound = lax.cond(excess, tie_bound, lambda: jnp.full((1, tq), 1 << nbits, jnp.int32))

    sel = ((key > thr) | ((key == thr) & (kpos < bound))) & valid
    selbias = jnp.where(sel, 0.0, NEG)

    k = k_ref[0]
    vt = vt_ref[0]

    def qk(h):
        return _dot(k, qt_ref[0, h * DH_B:(h + 1) * DH_B, :])

    s_next = qk(0)
    for h in range(H_B):
        s = s_next + selbias
        if h + 1 < H_B:
            s_next = qk(h + 1)
        m = jnp.max(s, axis=0, keepdims=True)
        p = jnp.exp2(s - m)
        l = jnp.sum(p, axis=0, keepdims=True)
        ot = _dot(vt, p.astype(BF16)) / l
        o_ref[0, :, h * DH_B:(h + 1) * DH_B] = ot.T.astype(BF16)


def _dsa_attention(qt, qit, wit, k, vt, ki, *, l_valid, q_off, n_sel, groups):
    b, nq, nf, tq = qt.shape
    lpad = k.shape[1]
    nqg = nq // groups
    tg = nqg * tq
    outs = []
    for gi in range(groups):
        q_end = q_off + (gi + 1) * tg - 1
        k_end = min(l_valid, ((q_end >> CHUNK_SHIFT) + 1) << CHUNK_SHIFT)
        lk = min(lpad, _round_up(k_end, LANES))
        qmap = lambda bi, i, gi=gi: (bi, gi * nqg + i, 0, 0)
        kmap = lambda bi, i: (bi, 0, 0)
        outs.append(pl.pallas_call(
            functools.partial(_dsa_kernel, tq=tq, lk=lk, l_valid=l_valid, q_off=q_off + gi * tg, n_sel=n_sel),
            grid=(b, nqg),
            in_specs=[
                _tiled_spec(nf, tq, qmap),
                _tiled_spec(qit.shape[2], tq, qmap),
                _tiled_spec(wit.shape[2], tq, qmap),
                pl.BlockSpec((1, lk, DH_B), kmap),
                pl.BlockSpec((1, DH_B, lk), kmap),
                pl.BlockSpec((1, lk, DH_IDX), kmap),
            ],
            out_specs=pl.BlockSpec((1, tq, nf), lambda bi, i: (bi, i, 0)),
            out_shape=jax.ShapeDtypeStruct((b, tg, nf), BF16),
            compiler_params=_cparams(("parallel", "parallel")),
            name="dsa_attention",
        )(qt, qit, wit, k, vt, ki))
    return outs[0] if groups == 1 else jnp.concatenate(outs, axis=1)


def _proj_c_kernel(x_ref, g_ref, wcq_ref, wckv_ref, wkpe_ref, gq_ref, gkv_ref, wuqt_ref,
                   ck_ref, sk_ref, ct_ref, st_ref, qt_ref, ckv_ref, kpe_ref):
    h = _rms(x_ref[...], g_ref[...]).astype(BF16)
    cq = _rms(_dot(h, wcq_ref[...]), gq_ref[...]).astype(BF16)
    ckv_ref[...] = _rms(_dot(h, wckv_ref[...]), gkv_ref[...])
    kpe_ref[...] = _rope_slab(_dot(h, wkpe_ref[...]), ck_ref[...], sk_ref[...], QK_ROPE)
    qt = _dot_nt(wuqt_ref[...], cq)
    cos, sin = ct_ref[...], st_ref[...]
    scale = (QK_NOPE + QK_ROPE) ** -0.5 * LOG2E
    half = QK_ROPE // 2
    for hh in range(H_C):
        base = hh * LANES
        r0 = base + QK_NOPE
        o1, o2 = _rope_rows(qt[r0:r0 + half, :], qt[r0 + half:r0 + QK_ROPE, :], cos, sin)
        qt_ref[0, base:r0, :] = (qt[base:r0, :] * scale).astype(BF16)
        qt_ref[0, r0:r0 + QK_ROPE, :] = (jnp.concatenate([o1, o2], axis=0) * scale).astype(BF16)
        qt_ref[0, r0 + QK_ROPE:base + LANES, :] = jnp.zeros((LANES - QK_NOPE - QK_ROPE, qt.shape[1]), BF16)


def _proj_c(x, g, w, gq, gkv, wuqt, row_tabs, time_tabs, b, t, tm=256):
    m, d = x.shape
    tm = min(tm, t)
    nt = t // tm
    in_specs = [pl.BlockSpec((tm, d), _row), _resident((1, d))]
    in_specs += [_resident(a.shape) for a in w]
    in_specs += [_resident((1, Q_LORA)), _resident((1, KV_LORA)), _resident(wuqt.shape)]
    in_specs += [pl.BlockSpec((tm, LANES), _row_tab(nt)) for _ in row_tabs]
    in_specs += [pl.BlockSpec((tb.shape[0], tm), _time_tab(nt)) for tb in time_tabs]
    nq = H_C * LANES
    return pl.pallas_call(
        _proj_c_kernel,
        grid=(m // tm,),
        in_specs=in_specs,
        out_specs=[_tiled_spec(nq, tm, _tiled_out(nt)), pl.BlockSpec((tm, KV_LORA), _row),
                   pl.BlockSpec((tm, LANES), _row)],
        out_shape=[jax.ShapeDtypeStruct((b, nt, nq, tm), BF16), jax.ShapeDtypeStruct((m, KV_LORA), F32),
                   jax.ShapeDtypeStruct((m, LANES), F32)],
        compiler_params=_cparams(("parallel",)),
        name="proj_c",
    )(x, g.reshape(1, d), *w, gq.reshape(1, Q_LORA), gkv.reshape(1, KV_LORA), wuqt, *row_tabs, *time_tabs)


def _kv_expand_kernel(ckv_ref, kpe_ref, wuk_ref, wuvt_ref, place_ref, k_ref, vt_ref):
    ckv = ckv_ref[0].astype(BF16)
    kpe = _dot(kpe_ref[0].astype(BF16), place_ref[...])
    kn = _dot(ckv, wuk_ref[...])
    for hh in range(H_C):
        sl = slice(hh * LANES, (hh + 1) * LANES)
        k_ref[0, :, sl] = (kn[:, sl] + kpe).astype(BF16)
    _store_aug_values(vt_ref, _dot_nt(wuvt_ref[...], ckv).astype(BF16))


def _kv_expand(ckv, kpe, wuk, wuvt, place, tm):
    b, l, _ = ckv.shape
    rmap = lambda bi, i: (bi, i, 0)
    cmap = lambda bi, i: (0, 0)
    return pl.pallas_call(
        _kv_expand_kernel,
        grid=(b, l // tm),
        in_specs=[pl.BlockSpec((1, tm, KV_LORA), rmap), pl.BlockSpec((1, tm, LANES), rmap),
                  pl.BlockSpec(wuk.shape, cmap), pl.BlockSpec(wuvt.shape, cmap),
                  pl.BlockSpec(place.shape, cmap)],
        out_specs=[pl.BlockSpec((1, tm, H_C * LANES), rmap),
                   _tiled_spec(H_C * VROWS, tm, lambda bi, i: (bi, i, 0, 0))],
        out_shape=[jax.ShapeDtypeStruct((b, l, H_C * LANES), BF16),
                   jax.ShapeDtypeStruct((b, l // tm, H_C * VROWS, tm), BF16)],
        compiler_params=_cparams(("parallel", "parallel")),
        name="kv_expand",
    )(ckv, kpe, wuk, wuvt, place)


def _rope_angles(pos, dim):
    half = dim // 2
    inv = ROPE_THETA ** (-jnp.arange(half, dtype=F32) / half)
    ang = pos.astype(F32)[:, None] * inv[None, :]
    return jnp.cos(ang), jnp.sin(ang)


def _rope_row_tables(pos, dim):
    cos, sin = _rope_angles(pos, dim)
    reps = LANES // dim
    return (jnp.tile(jnp.concatenate([cos, cos], axis=-1), (1, reps)),
            jnp.tile(jnp.concatenate([-sin, sin], axis=-1), (1, reps)))


def _rope_time_tables(pos, dim):
    cos, sin = _rope_angles(pos, dim)
    return cos.T, sin.T


def _prep_weights(p):
    w = {}
    w["ffn1_in"] = p["ffn1_w_in"].astype(BF16)
    w["ffn1_out"] = p["ffn1_w_out"].astype(BF16)
    w["ffn2_in"] = p["ffn2_w_in"].astype(BF16)
    w["ffn2_out"] = p["ffn2_w_out"].astype(BF16)
    a = p["a_w_in"].astype(BF16)
    n = H_A * DH_A
    w["a"] = (a[:, :n].T, a[:, n:2 * n], a[:, 2 * n:3 * n], a[:, 2 * n:3 * n].T, a[:, 3 * n:])
    w["a_out"] = p["a_w_out"].astype(BF16)
    bw = p["b_w_in"].astype(BF16)
    offs = np.cumsum([0, H_B * DH_B, DH_B, DH_B, H_IDX * DH_IDX, DH_IDX, H_IDX]).tolist()
    bq, bk, bv, bqi, bki, bwi = [bw[:, offs[i]:offs[i + 1]] for i in range(6)]
    w["b"] = (bq.T, bk, bv, bv.T, bqi.T, jnp.concatenate([bki, bki], axis=1), bwi.T)
    w["b_out"] = p["b_w_out"].astype(BF16)
    c = p["c_w_down"].astype(BF16)
    d = c.shape[0]
    wkpe = jnp.zeros((d, LANES), BF16).at[:, :QK_ROPE].set(c[:, Q_LORA + KV_LORA:])
    w["c"] = (c[:, :Q_LORA], c[:, Q_LORA:Q_LORA + KV_LORA], wkpe)
    uq = p["c_w_uq"].astype(BF16).reshape(Q_LORA, H_C, QK_NOPE + QK_ROPE)
    uq = jnp.pad(uq, ((0, 0), (0, 0), (0, LANES - QK_NOPE - QK_ROPE)))
    w["c_uqt"] = uq.reshape(Q_LORA, H_C * LANES).T
    ukv = p["c_w_ukv"].astype(BF16).reshape(KV_LORA, H_C, QK_NOPE + V_C)
    uk = jnp.pad(ukv[:, :, :QK_NOPE], ((0, 0), (0, 0), (0, LANES - QK_NOPE)))
    w["c_uk"] = uk.reshape(KV_LORA, H_C * LANES)
    w["c_uvt"] = ukv[:, :, QK_NOPE:].reshape(KV_LORA, H_C * V_C).T
    w["c_place"] = jnp.zeros((LANES, LANES), BF16).at[jnp.arange(QK_ROPE), QK_NOPE + jnp.arange(QK_ROPE)].set(1.0)
    w["c_out"] = p["c_w_out"].astype(BF16)
    dw = p["d_w_in"].astype(BF16)
    n = H_D * DH_D
    w["d"] = (dw[:, :n].T, dw[:, n:2 * n], dw[:, 2 * n:], dw[:, 2 * n:].T)
    w["d_out"] = p["d_w_out"].astype(BF16)
    return w


def _round_up(x, m):
    return (x + m - 1) // m * m


def _cat_time(past, new, lpad, axis):
    a = new if past is None else jnp.concatenate([past.astype(new.dtype), new], axis=axis)
    if a.shape[axis] < lpad:
        pad = [(0, 0)] * a.ndim
        pad[axis] = (0, lpad - a.shape[axis])
        a = jnp.pad(a, pad)
    return a


def _trunk(x3, past, p, w):
    b, t, d = x3.shape
    m = b * t
    x = x3.reshape(m, d)
    past_len = 0 if past is None else past[0][0].shape[1]
    l_valid = past_len + t
    pos = past_len + jnp.arange(t, dtype=jnp.int32)
    if past is None:
        tile = min(256, t)
        tile_s = min(512, t)
        lpad = _round_up(l_valid, tile_s)
        ktile, ktile_s, tpad = tile, tile_s, t
        groups = 4 if t % (4 * tile) == 0 else 1
    else:
        tile = tile_s = tpad = _round_up(t, LANES)
        lpad = ktile_s = _round_up(l_valid, LANES)
        ktile = LANES
        groups = 1
    norm_g = p["norm_g"]
    rows = {}

    def r3(a):
        return a.reshape(b, t, a.shape[-1])

    def q_in(a4):
        return a4 if tpad == t else jnp.pad(a4, ((0, 0), (0, 0), (0, 0), (0, tpad - t)))

    def v_in(vt4, past_t, kt):
        if past is None:
            return vt4
        a = _cat_time(past_t, vt4.reshape(b, vt4.shape[2], t), lpad, 2)
        return jnp.transpose(a.reshape(b, a.shape[1], lpad // kt, kt), (0, 2, 1, 3))

    def unpad(o):
        return (o if tpad == t else o[:, :t]).reshape(m, -1)

    def pst(idx, j, width):
        return None if past is None else past[idx][j].reshape(b, past_len, width)

    def pst_t(idx, j, width):
        return None if past is None else jnp.transpose(past[idx][j].reshape(b, past_len, width), (0, 2, 1))

    def aug(vt_past):
        if vt_past is None:
            return None
        dv = LANES // 2
        v4 = vt_past.reshape(b, -1, dv, past_len)
        extra = jnp.zeros((b, v4.shape[1], VROWS - dv, past_len), v4.dtype).at[:, :, 0, :].set(1.0)
        return jnp.concatenate([v4, extra], axis=2).reshape(b, -1, past_len)

    for i in range(4):
        x = _ffn(x, norm_g[i, 0], w["ffn1_in"][i], w["ffn1_out"][i])
        g = norm_g[i, 1]
        if i == 0:
            wqt, wk, wv, wvt, wf = w["a"]
            qt, k, kb, v, vt, lf = _proj_qkv(x, g, wqt, wk, wv, wvt, DH_A ** -0.5 * LOG2E, b, t, wf, p["a_b_f"],
                                             tm=tile_s)
            rows["a"] = (k.reshape(b, t, H_A, DH_A), v.reshape(b, t, H_A, DH_A), lf.reshape(b, t, H_A))
            lf_all = _cat_time(None if past is None else past[0][2].astype(F32), r3(lf), lpad, 1)
            c = _cumsum_time(jnp.pad(lf_all, ((0, 0), (0, 0), (0, LANES - H_A))))
            o = _attention(q_in(qt), _cat_time(pst(0, 0, H_A * DH_A), r3(kb), lpad, 1),
                           v_in(vt, aug(pst_t(0, 1, H_A * DH_A)), ktile_s), c,
                           nh=H_A, k_per_head=False, chunked=False, q_off=past_len, l_valid=l_valid)
            mix = (unpad(o), w["a_out"])
        elif i == 1:
            row_tabs = list(_rope_row_tables(pos, DH_B)) + list(_rope_row_tables(pos, DH_IDX))
            time_tabs = list(_rope_time_tables(pos, DH_B)) + list(_rope_time_tables(pos, DH_IDX))
            qt, k, kb, v, vt, qit, ki, kib, wit = _proj_b(x, g, w["b"], row_tabs, time_tabs, b, t, tm=tile)
            rows["b"] = (r3(k), r3(v), r3(ki))
            n_sel = min(IDX_TOPK, l_valid // 4)
            o = _dsa_attention(q_in(qt), q_in(qit), q_in(wit), _cat_time(pst(1, 0, DH_B), r3(kb), lpad, 1),
                               _cat_time(pst_t(1, 1, DH_B), vt, lpad, 2),
                               _cat_time(pst(1, 2, DH_IDX), r3(kib), lpad, 1),
                               l_valid=l_valid, q_off=past_len, n_sel=n_sel, groups=groups)
            mix = (unpad(o), w["b_out"])
        elif i == 2:
            row_tabs = list(_rope_row_tables(pos, QK_ROPE))
            time_tabs = list(_rope_time_tables(pos, QK_ROPE))
            qt, ckv, kpe = _proj_c(x, g, w["c"], p["c_g_q"], p["c_g_kv"], w["c_uqt"], row_tabs, time_tabs, b, t,
                                   tm=tile_s)
            rows["c"] = (r3(ckv), r3(kpe)[:, :, :QK_ROPE])
            kpe_past = None if past is None else jnp.pad(past[2][1].astype(F32),
                                                         ((0, 0), (0, 0), (0, LANES - QK_ROPE)))
            ckv_all = _cat_time(None if past is None else past[2][0].astype(F32), r3(ckv), lpad, 1)
            kpe_all = _cat_time(kpe_past, r3(kpe), lpad, 1)
            kc, vct = _kv_expand(ckv_all, kpe_all, w["c_uk"], w["c_uvt"], w["c_place"], tm=ktile_s)
            o = _attention(q_in(qt), kc, vct, None,
                           nh=H_C, k_per_head=True, chunked=True, q_off=past_len, l_valid=l_valid)
            mix = (unpad(o), w["c_out"])
        else:
            wqt, wk, wv, wvt = w["d"]
            qt, k, kb, v, vt = _proj_qkv(x, g, wqt, wk, wv, wvt, DH_D ** -0.5 * LOG2E, b, t, tm=tile)
            rows["d"] = (k.reshape(b, t, H_D, DH_D), v.reshape(b, t, H_D, DH_D))
            o = _sb_attention(q_in(qt), _cat_time(pst(3, 0, H_D * DH_D), r3(kb), lpad, 1),
                              v_in(vt, pst_t(3, 1, H_D * DH_D), ktile), nh=H_D, q_off=past_len)
            mix = (unpad(o), w["d_out"])
        x = _ffn(x, norm_g[i, 2], w["ffn2_in"][i], w["ffn2_out"][i], mix, p["final_g"] if i == 3 else None)
    return x.reshape(b, t, d), rows


def kernel(x_prompt, x_sample, cache_a_k, cache_a_v, cache_a_logf, cache_b_k, cache_b_v, cache_b_kidx,
           cache_c_ckv, cache_c_kpe, cache_d_k, cache_d_v, norm_g, final_g,
           ffn1_w_in, ffn1_w_out, ffn2_w_in, ffn2_w_out, a_w_in, a_b_f, a_w_out, b_w_in, b_w_out,
           c_w_down, c_g_q, c_g_kv, c_w_uq, c_w_ukv, c_w_out, d_w_in, d_w_out):
    p = dict(norm_g=norm_g, final_g=final_g, ffn1_w_in=ffn1_w_in, ffn1_w_out=ffn1_w_out,
             ffn2_w_in=ffn2_w_in, ffn2_w_out=ffn2_w_out, a_w_in=a_w_in, a_b_f=a_b_f, a_w_out=a_w_out,
             b_w_in=b_w_in, b_w_out=b_w_out, c_w_down=c_w_down, c_g_q=c_g_q, c_g_kv=c_g_kv,
             c_w_uq=c_w_uq, c_w_ukv=c_w_ukv, c_w_out=c_w_out, d_w_in=d_w_in, d_w_out=d_w_out)
    w = _prep_weights(p)
    y_prompt, pr = _trunk(x_prompt, None, p, w)
    past = ((cache_a_k, cache_a_v, cache_a_logf), (cache_b_k, cache_b_v, cache_b_kidx),
            (cache_c_ckv, cache_c_kpe), (cache_d_k, cache_d_v))
    y_sample, sr = _trunk(x_sample, past, p, w)
    return (y_prompt, y_sample, *pr["a"], *pr["b"], *pr["c"], *pr["d"],
            *sr["a"], *sr["b"], *sr["c"], *sr["d"])
```

```python
import functools

import jax
import jax.numpy as jnp
import numpy as np
from jax import lax
from jax.experimental import pallas as pl
from jax.experimental.pallas import tpu as pltpu

F32 = jnp.float32
BF16 = jnp.bfloat16

D_MODEL = 1024
CHUNK_SHIFT = 6
ROPE_THETA = 10000.0
RMS_EPS = 1e-6
D_FF = 2816
H_A, DH_A = 16, 64
H_B, DH_B = 8, 128
H_IDX, DH_IDX = 8, 64
IDX_TOPK = 256
H_C, Q_LORA, KV_LORA, QK_NOPE, QK_ROPE, V_C = 16, 256, 128, 64, 32, 64
H_D, DH_D = 16, 64

LANES = 128
NEG = -1e30
LOG2E = 1.4426950408889634
EXP2_ZERO = -150.0
VROWS = 80
VMEM_LIMIT = 56 * 1024 * 1024


def _cparams(sem):
    return pltpu.CompilerParams(dimension_semantics=sem, vmem_limit_bytes=VMEM_LIMIT)


def _dot(a, b):
    return jnp.dot(a, b, preferred_element_type=F32)


def _dot_nt(a, b):
    return lax.dot_general(a, b, (((1,), (1,)), ((), ())), preferred_element_type=F32)


def _rms(x, g):
    return x * lax.rsqrt(jnp.mean(x * x, axis=-1, keepdims=True) + RMS_EPS) * g


def _softplus_neg_abs(z):
    return jnp.log(1.0 + jnp.exp(-jnp.abs(z)))


def _log_sigmoid(z):
    return jnp.minimum(z, 0.0) - _softplus_neg_abs(z)


def _rope_slab(x, cos, sin, group):
    half = group // 2
    if group == LANES:
        partner = pltpu.roll(x, half, 1)
    else:
        lane = lax.broadcasted_iota(jnp.int32, (1, LANES), 1)
        first = (lane & (group - 1)) < half
        partner = jnp.where(first, pltpu.roll(x, LANES - half, 1), pltpu.roll(x, half, 1))
    return x * cos + partner * sin


def _store_aug_values(vt_ref, vt):
    dv = LANES // 2
    n = vt.shape[1]
    extra = jnp.where(lax.broadcasted_iota(jnp.int32, (VROWS - dv, n), 0) == 0, 1.0, 0.0).astype(BF16)
    for hh in range(vt.shape[0] // dv):
        vt_ref[0, hh * VROWS:hh * VROWS + dv, :] = vt[hh * dv:(hh + 1) * dv, :]
        vt_ref[0, hh * VROWS + dv:(hh + 1) * VROWS, :] = extra


def _rope_rows(x1, x2, cos, sin):
    return x1 * cos - x2 * sin, x1 * sin + x2 * cos


FFN_CHUNK = 256


def _ffn_kernel(x_ref, *rest, mix, final):
    rest = list(rest)
    if mix:
        o_ref, wmix_ref = rest.pop(0), rest.pop(0)
    g_ref, win_ref, wo_ref = rest.pop(0), rest.pop(0), rest.pop(0)
    if final:
        fg_ref = rest.pop(0)
    (y_ref,) = rest
    dff = wo_ref.shape[0]
    x = x_ref[...]
    if mix:
        x = x + _dot(o_ref[...], wmix_ref[...])
    h = _rms(x, g_ref[...]).astype(BF16)

    def inproj(c):
        lo = c * FFN_CHUNK
        return _dot(h, win_ref[:, lo:lo + FFN_CHUNK]), _dot(h, win_ref[:, dff + lo:dff + lo + FFN_CHUNK])

    acc = None
    nxt = inproj(0)
    for c in range(dff // FFN_CHUNK):
        up, gate = nxt
        if (c + 1) * FFN_CHUNK < dff:
            nxt = inproj(c + 1)
        act = (gate / (1.0 + jnp.exp(-gate)) * up).astype(BF16)
        part = _dot(act, wo_ref[c * FFN_CHUNK:(c + 1) * FFN_CHUNK, :])
        acc = part if acc is None else acc + part
    y = x + 0.5 * acc
    if final:
        y = _rms(y, fg_ref[...])
    y_ref[...] = y


def _resident(shape):
    return pl.BlockSpec(shape, lambda i: (0,) * len(shape), pipeline_mode=pl.Buffered(1))


def _ffn(x, g, w_in, w_out, mix=None, final_g=None, tm=512):
    m, d = x.shape
    dff = w_out.shape[0]
    tm = min(tm, m)
    row = pl.BlockSpec((tm, d), lambda i: (i, 0))
    in_specs, args = [row], [x]
    if mix is not None:
        o, wmix = mix
        in_specs += [pl.BlockSpec((tm, o.shape[1]), lambda i: (i, 0)), _resident(wmix.shape)]
        args += [o, wmix]
    in_specs += [_resident((1, d)), _resident((d, 2 * dff)), _resident((dff, d))]
    args += [g.reshape(1, d), w_in, w_out]
    if final_g is not None:
        in_specs.append(_resident((1, d)))
        args.append(final_g.reshape(1, d))
    return pl.pallas_call(
        functools.partial(_ffn_kernel, mix=mix is not None, final=final_g is not None),
        grid=(m // tm,),
        in_specs=in_specs,
        out_specs=row,
        out_shape=jax.ShapeDtypeStruct((m, d), F32),
        compiler_params=_cparams(("parallel",)),
        name="ffn",
    )(*args)


def _row(i):
    return (i, 0)


def _feat_time(nt):
    return lambda i: (i // nt, 0, i % nt)


def _tiled_spec(feat, tile, index_map):
    return pl.BlockSpec((None, 1, feat, tile), index_map)


def _tiled_out(nt):
    return lambda i: (i // nt, i % nt, 0, 0)


def _time_tab(nt):
    return lambda i: (0, i % nt)


def _row_tab(nt):
    return lambda i: (i % nt, 0)


def _proj_qkv_kernel(x_ref, g_ref, wqt_ref, wk_ref, wv_ref, wvt_ref, *rest, scale, has_f):
    if has_f:
        wf_ref, bf_ref, qt_ref, k_ref, kb_ref, v_ref, vt_ref, lf_ref = rest
    else:
        qt_ref, k_ref, kb_ref, v_ref, vt_ref = rest
    h = _rms(x_ref[...], g_ref[...]).astype(BF16)
    qt = (_dot_nt(wqt_ref[...], h) * scale).astype(BF16)
    half = LANES // 2
    zeros = jnp.zeros((half, qt.shape[1]), BF16)
    for hh in range(qt.shape[0] // half):
        base = hh * LANES
        own = slice(base + (hh % 2) * half, base + (hh % 2) * half + half)
        other = slice(base + (1 - hh % 2) * half, base + (1 - hh % 2) * half + half)
        qt_ref[0, own, :] = qt[hh * half:(hh + 1) * half, :]
        qt_ref[0, other, :] = zeros
    k = _dot(h, wk_ref[...])
    k_ref[...] = k.reshape(k_ref.shape)
    kb_ref[...] = k.astype(BF16)
    v_ref[...] = _dot(h, wv_ref[...]).reshape(v_ref.shape)
    vt = _dot_nt(wvt_ref[...], h).astype(BF16)
    if has_f:
        _store_aug_values(vt_ref, vt)
        lf_ref[...] = _log_sigmoid(_dot(h, wf_ref[...]) + bf_ref[...])
    else:
        vt_ref[0] = vt


def _proj_qkv(x, g, wqt, wk, wv, wvt, scale, b, t, wf=None, bf=None, tm=256):
    m, d = x.shape
    n = wk.shape[1]
    tm = min(tm, t)
    nt = t // tm
    has_f = wf is not None
    nvt = n // (LANES // 2) * VROWS if has_f else n
    in_specs = [pl.BlockSpec((tm, d), _row), _resident((1, d))] + [_resident((d, n))] * 4
    args = [x, g.reshape(1, d), wqt, wk, wv, wvt]
    dh = LANES // 2
    heads = pl.BlockSpec((tm, n // dh, dh), lambda i: (i, 0, 0))
    out_specs = [_tiled_spec(2 * n, tm, _tiled_out(nt)), heads,
                 pl.BlockSpec((tm, n), _row), heads, _tiled_spec(nvt, tm, _tiled_out(nt))]
    out_shape = [jax.ShapeDtypeStruct((b, nt, 2 * n, tm), BF16), jax.ShapeDtypeStruct((m, n // dh, dh), F32),
                 jax.ShapeDtypeStruct((m, n), BF16), jax.ShapeDtypeStruct((m, n // dh, dh), F32),
                 jax.ShapeDtypeStruct((b, nt, nvt, tm), BF16)]
    if has_f:
        nh = wf.shape[1]
        in_specs += [_resident((d, nh)), _resident((1, nh))]
        args += [wf, bf.reshape(1, nh)]
        out_specs.append(pl.BlockSpec((tm, nh), _row))
        out_shape.append(jax.ShapeDtypeStruct((m, nh), F32))
    return pl.pallas_call(
        functools.partial(_proj_qkv_kernel, scale=scale, has_f=has_f),
        grid=(m // tm,),
        in_specs=in_specs,
        out_specs=out_specs,
        out_shape=out_shape,
        compiler_params=_cparams(("parallel",)),
        name="proj_qkv",
    )(*args)


def _split3(x):
    hi = x.astype(BF16)
    r1 = x - hi.astype(F32)
    mid = r1.astype(BF16)
    lo = (r1 - mid.astype(F32)).astype(BF16)
    return hi, mid, lo


def _cumsum_kernel(x_ref, c_ref, *, nblk):
    r = lax.broadcasted_iota(jnp.int32, (LANES, LANES), 0)
    c = lax.broadcasted_iota(jnp.int32, (LANES, LANES), 1)
    tri = jnp.where(c <= r, 1.0, 0.0).astype(BF16)
    carry = jnp.zeros((1, x_ref.shape[2]), F32)
    for b in range(nblk):
        hi, mid, lo = _split3(x_ref[0, b * LANES:(b + 1) * LANES, :])
        blk = (_dot(tri, hi) + _dot(tri, mid)) + _dot(tri, lo) + carry
        c_ref[0, b * LANES:(b + 1) * LANES, :] = blk
        carry = blk[LANES - 1:LANES, :]


def _cumsum_time(x):
    b, l, nh = x.shape
    spec = pl.BlockSpec((1, l, nh), lambda i: (i, 0, 0))
    return pl.pallas_call(
        functools.partial(_cumsum_kernel, nblk=l // LANES),
        grid=(b,),
        in_specs=[spec],
        out_specs=spec,
        out_shape=jax.ShapeDtypeStruct((b, l, nh), F32),
        compiler_params=_cparams(("parallel",)),
        name="cumsum_time",
    )(x)


def _last_block(i, tq, tk, q_off, chunked, nk):
    q_last = q_off + (i + 1) * tq - 1
    if chunked:
        q_last = ((q_last >> CHUNK_SHIFT) << CHUNK_SHIFT) + (1 << CHUNK_SHIFT) - 1
    return jnp.minimum(q_last // tk, nk - 1)


def _attn_kernel(qt_ref, k_ref, vt_ref, *rest, nh, k_per_head, bias, chunked, tq, tk, q_off, l_valid, nk):
    if bias:
        c_ref, o_ref, m_ref, acc_ref = rest
    else:
        o_ref, m_ref, acc_ref = rest
    i = pl.program_id(1)
    j = pl.program_id(2)
    dv = LANES // 2

    @pl.when(j == 0)
    def _():
        m_ref[...] = jnp.full_like(m_ref, NEG)
        acc_ref[...] = jnp.zeros_like(acc_ref)

    q_first = q_off + i * tq
    k_last = j * tk + tk - 1
    if chunked:
        full = ((k_last >> CHUNK_SHIFT) <= (q_first >> CHUNK_SHIFT)) & (k_last < l_valid)
    else:
        full = k_last <= q_first
    active = j <= _last_block(i, tq, tk, q_off, chunked, nk)

    def step(masked):
        if masked:
            kpos = j * tk + lax.broadcasted_iota(jnp.int32, (tk, tq), 0)
            qpos = q_first + lax.broadcasted_iota(jnp.int32, (tk, tq), 1)
            if chunked:
                valid = ((kpos >> CHUNK_SHIFT) <= (qpos >> CHUNK_SHIFT)) & (kpos < l_valid)
            else:
                valid = kpos <= qpos
            maskbias = jnp.where(valid, 0.0, NEG)
        def qk(h):
            ks = h if k_per_head else h // 2
            return _dot(k_ref[0, :, ks * LANES:(ks + 1) * LANES], qt_ref[0, h * LANES:(h + 1) * LANES, :])

        s_next = qk(0)
        for h in range(nh):
            s = s_next
            if h + 1 < nh:
                s_next = qk(h + 1)
            if bias:
                s = s - c_ref[0, :, h:h + 1] * LOG2E
            if masked:
                s = s + maskbias
            m_prev = m_ref[h:h + 1, :]
            m_new = jnp.maximum(m_prev, jnp.max(s, axis=0, keepdims=True))
            alpha = jnp.exp2(m_prev - m_new)
            p = jnp.exp2(s - m_new)
            m_ref[h:h + 1, :] = m_new
            rows = slice(h * VROWS, (h + 1) * VROWS)
            acc_ref[rows, :] = alpha * acc_ref[rows, :] + _dot(vt_ref[0, rows, :], p.astype(BF16))

    @pl.when(active & full)
    def _():
        step(False)

    @pl.when(active & jnp.logical_not(full))
    def _():
        step(True)

    @pl.when(j == nk - 1)
    def _():
        for p in range(nh // 2):
            parts = [acc_ref[h * VROWS:h * VROWS + dv, :] / acc_ref[h * VROWS + dv:h * VROWS + dv + 1, :]
                     for h in (2 * p, 2 * p + 1)]
            o_ref[0, :, p * LANES:(p + 1) * LANES] = jnp.concatenate(parts, axis=0).T.astype(BF16)


def _attention(qt, k, vt, c, *, nh, k_per_head, chunked, q_off, l_valid):
    b, nq, _, tq = qt.shape
    nk, tk = vt.shape[1], vt.shape[3]
    assert k.shape[1] == nk * tk
    bias = c is not None

    def kblk(i, j):
        return jnp.minimum(j, _last_block(i, tq, tk, q_off, chunked, nk))

    in_specs = [
        _tiled_spec(qt.shape[2], tq, lambda bi, i, j: (bi, i, 0, 0)),
        pl.BlockSpec((1, tk, k.shape[2]), lambda bi, i, j: (bi, kblk(i, j), 0)),
        _tiled_spec(vt.shape[2], tk, lambda bi, i, j: (bi, kblk(i, j), 0, 0)),
    ]
    args = [qt, k, vt]
    if bias:
        in_specs.append(pl.BlockSpec((1, tk, c.shape[2]), lambda bi, i, j: (bi, kblk(i, j), 0)))
        args.append(c)
    dv = nh * (LANES // 2)
    return pl.pallas_call(
        functools.partial(_attn_kernel, nh=nh, k_per_head=k_per_head, bias=bias, chunked=chunked,
                          tq=tq, tk=tk, q_off=q_off, l_valid=l_valid, nk=nk),
        grid=(b, nq, nk),
        in_specs=in_specs,
        out_specs=pl.BlockSpec((1, tq, dv), lambda bi, i, j: (bi, i, 0)),
        out_shape=jax.ShapeDtypeStruct((b, nq * tq, dv), BF16),
        scratch_shapes=[pltpu.VMEM((nh, tq), F32), pltpu.VMEM((vt.shape[2], tq), F32)],
        compiler_params=_cparams(("parallel", "parallel", "arbitrary")),
        name="attention",
    )(*args)


def _sb_kernel(qt_ref, k_ref, vt_ref, o_ref, tail_ref, acc_ref, *, nh, tq, tk, q_off, nk):
    i = pl.program_id(1)
    j = pl.program_id(2)
    dv = LANES // 2

    @pl.when(j == 0)
    def _():
        tail_ref[...] = jnp.zeros_like(tail_ref)
        acc_ref[...] = jnp.zeros_like(acc_ref)

    last = _last_block(i, tq, tk, q_off, False, nk)
    jj = last - j
    q_first = q_off + i * tq
    full = jj * tk + tk - 1 < q_first
    active = (j <= last) & (jnp.max(tail_ref[...]) > EXP2_ZERO)

    def step(masked):
        r = lax.broadcasted_iota(jnp.int32, (tk + 16, tk), 0)
        c = lax.broadcasted_iota(jnp.int32, (tk + 16, tk), 1)
        after = jnp.where(((r < tk) & (c > r)) | (r == tk), 1.0, 0.0).astype(BF16)
        if masked:
            kpos = jj * tk + lax.broadcasted_iota(jnp.int32, (tk, tq), 0)
            qpos = q_first + lax.broadcasted_iota(jnp.int32, (tk, tq), 1)
            valid = kpos < qpos

        def qk(h):
            return _dot(k_ref[0, :, (h // 2) * LANES:(h // 2 + 1) * LANES], qt_ref[0, h * LANES:(h + 1) * LANES, :])

        def logs(z):
            log_beta = jnp.minimum(z, 0.0) - jnp.log2(1.0 + jnp.exp2(-jnp.abs(z)))
            log_1mb = log_beta - z
            if masked:
                log_1mb = jnp.where(valid, log_1mb, 0.0)
            return log_beta, _dot(after, log_1mb.astype(BF16))

        def finish(h, log_beta, sums):
            t_prev = tail_ref[h:h + 1, :]
            a = jnp.exp2(log_beta + sums[:tk, :] + t_prev)
            if masked:
                a = jnp.where(valid, a, 0.0)
            tail_ref[h:h + 1, :] = t_prev + sums[tk:tk + 1, :]
            rows = slice(h * dv, (h + 1) * dv)
            acc_ref[rows, :] += _dot(vt_ref[0, rows, :], a.astype(BF16))

        z_next = qk(0)
        pending = None
        for h in range(nh):
            z = z_next
            if h + 1 < nh:
                z_next = qk(h + 1)
            cur = logs(z)
            if pending is not None:
                finish(h - 1, *pending)
            pending = cur
        finish(nh - 1, *pending)

    @pl.when(active & full)
    def _():
        step(False)

    @pl.when(active & jnp.logical_not(full))
    def _():
        step(True)

    @pl.when(j == nk - 1)
    def _():
        for p in range(nh // 2):
            o_ref[0, :, p * LANES:(p + 1) * LANES] = acc_ref[p * LANES:(p + 1) * LANES, :].T.astype(BF16)


def _sb_attention(qt, k, vt, *, nh, q_off):
    b, nq, _, tq = qt.shape
    nk, dv, tk = vt.shape[1:]
    assert k.shape[1] == nk * tk

    def kblk(i, j):
        return jnp.maximum(_last_block(i, tq, tk, q_off, False, nk) - j, 0)

    return pl.pallas_call(
        functools.partial(_sb_kernel, nh=nh, tq=tq, tk=tk, q_off=q_off, nk=nk),
        grid=(b, nq, nk),
        in_specs=[
            _tiled_spec(qt.shape[2], tq, lambda bi, i, j: (bi, i, 0, 0)),
            pl.BlockSpec((1, tk, k.shape[2]), lambda bi, i, j: (bi, kblk(i, j), 0)),
            _tiled_spec(dv, tk, lambda bi, i, j: (bi, kblk(i, j), 0, 0)),
        ],
        out_specs=pl.BlockSpec((1, tq, dv), lambda bi, i, j: (bi, i, 0)),
        out_shape=jax.ShapeDtypeStruct((b, nq * tq, dv), BF16),
        scratch_shapes=[pltpu.VMEM((nh, tq), F32), pltpu.VMEM((dv, tq), F32)],
        compiler_params=_cparams(("parallel", "parallel", "arbitrary")),
        name="sb_attention",
    )(qt, k, vt)


def _proj_b_kernel(x_ref, g_ref, wqt_ref, wk_ref, wv_ref, wvt_ref, wqit_ref, wki_ref, wwit_ref,
                   c128_ref, s128_ref, c64_ref, s64_ref, ct64_ref, st64_ref, ct32_ref, st32_ref,
                   qt_ref, k_ref, kb_ref, v_ref, vt_ref, qit_ref, ki_ref, kib_ref, wit_ref):
    h = _rms(x_ref[...], g_ref[...]).astype(BF16)
    qt = _dot_nt(wqt_ref[...], h)
    cos, sin = ct64_ref[...], st64_ref[...]
    half = DH_B // 2
    for hh in range(H_B):
        base = hh * DH_B
        o1, o2 = _rope_rows(qt[base:base + half, :], qt[base + half:base + DH_B, :], cos, sin)
        qt_ref[0, base:base + half, :] = (o1 * (DH_B ** -0.5 * LOG2E)).astype(BF16)
        qt_ref[0, base + half:base + DH_B, :] = (o2 * (DH_B ** -0.5 * LOG2E)).astype(BF16)
    k = _rope_slab(_dot(h, wk_ref[...]), c128_ref[...], s128_ref[...], DH_B)
    k_ref[...] = k
    kb_ref[...] = k.astype(BF16)
    v_ref[...] = _dot(h, wv_ref[...])
    vt_ref[0] = _dot_nt(wvt_ref[...], h).astype(BF16)
    qit = _dot_nt(wqit_ref[...], h)
    cos, sin = ct32_ref[...], st32_ref[...]
    half = DH_IDX // 2
    for hh in range(H_IDX):
        base = hh * DH_IDX
        o1, o2 = _rope_rows(qit[base:base + half, :], qit[base + half:base + DH_IDX, :], cos, sin)
        qit_ref[0, base:base + half, :] = (o1 * (DH_IDX ** -0.5)).astype(BF16)
        qit_ref[0, base + half:base + DH_IDX, :] = (o2 * (DH_IDX ** -0.5)).astype(BF16)
    ki = _rope_slab(_dot(h, wki_ref[...]), c64_ref[...], s64_ref[...], DH_IDX)[:, :DH_IDX]
    ki_ref[...] = ki
    kib_ref[...] = ki.astype(BF16)
    wit_ref[0] = _dot_nt(wwit_ref[...], h) * (H_IDX ** -0.5)


def _proj_b(x, g, w, row_tabs, time_tabs, b, t, tm=256):
    m, d = x.shape
    tm = min(tm, t)
    nt = t // tm
    in_specs = [pl.BlockSpec((tm, d), _row), _resident((1, d))]
    in_specs += [_resident(a.shape) for a in w]
    in_specs += [pl.BlockSpec((tm, LANES), _row_tab(nt)) for _ in row_tabs]
    in_specs += [pl.BlockSpec((tb.shape[0], tm), _time_tab(nt)) for tb in time_tabs]
    nq, ni = H_B * DH_B, H_IDX * DH_IDX
    out_specs = [_tiled_spec(nq, tm, _tiled_out(nt)), pl.BlockSpec((tm, DH_B), _row),
                 pl.BlockSpec((tm, DH_B), _row), pl.BlockSpec((tm, DH_B), _row),
                 pl.BlockSpec((1, DH_B, tm), _feat_time(nt)), _tiled_spec(ni, tm, _tiled_out(nt)),
                 pl.BlockSpec((tm, DH_IDX), _row), pl.BlockSpec((tm, DH_IDX), _row),
                 _tiled_spec(H_IDX, tm, _tiled_out(nt))]
    out_shape = [jax.ShapeDtypeStruct((b, nt, nq, tm), BF16), jax.ShapeDtypeStruct((m, DH_B), F32),
                 jax.ShapeDtypeStruct((m, DH_B), BF16), jax.ShapeDtypeStruct((m, DH_B), F32),
                 jax.ShapeDtypeStruct((b, DH_B, t), BF16), jax.ShapeDtypeStruct((b, nt, ni, tm), BF16),
                 jax.ShapeDtypeStruct((m, DH_IDX), F32), jax.ShapeDtypeStruct((m, DH_IDX), BF16),
                 jax.ShapeDtypeStruct((b, nt, H_IDX, tm), F32)]
    return pl.pallas_call(
        _proj_b_kernel,
        grid=(m // tm,),
        in_specs=in_specs,
        out_specs=out_specs,
        out_shape=out_shape,
        compiler_params=_cparams(("parallel",)),
        name="proj_b",
    )(x, g.reshape(1, d), *w, *row_tabs, *time_tabs)


def _dsa_kernel(qt_ref, qit_ref, wit_ref, k_ref, vt_ref, ki_ref, o_ref, *, tq, lk, l_valid, q_off, n_sel):
    i = pl.program_id(1)
    kpos = lax.broadcasted_iota(jnp.int32, (lk, tq), 0)
    qpos = q_off + i * tq + lax.broadcasted_iota(jnp.int32, (lk, tq), 1)
    valid = ((kpos >> CHUNK_SHIFT) <= (qpos >> CHUNK_SHIFT)) & (kpos < l_valid)

    ki = ki_ref[0]
    score = jnp.zeros((lk, tq), F32)
    for h in range(H_IDX):
        d = _dot(ki, qit_ref[0, h * DH_IDX:(h + 1) * DH_IDX, :])
        score = score + wit_ref[0, h:h + 1, :] * jnp.maximum(d, 0.0)
    score = jnp.where(valid, score + 0.0, -jnp.inf)

    bits = pltpu.bitcast(score, jnp.int32)
    key = jnp.where(bits < 0, bits ^ jnp.int32(0x7FFFFFFF), bits)

    int_min = jnp.int32(-2 ** 31)
    nsel_f = jnp.float32(n_sel)

    def count(pred):
        return jnp.sum(jnp.where(pred, 1.0, 0.0), axis=0, keepdims=True)

    def thr_body(it, t):
        cand = t + lax.shift_left(jnp.int32(1), jnp.int32(31) - it)
        return jnp.where(count(key >= cand) >= nsel_f, cand, t)

    thr = lax.fori_loop(0, 32, thr_body, jnp.full((1, tq), int_min, jnp.int32))

    nbits = int(lk).bit_length()

    def tie_bound():
        need = nsel_f - count(key > thr)

        def tie_body(it, bound):
            cand = bound + lax.shift_left(jnp.int32(1), jnp.int32(nbits - 1) - it)
            return jnp.where(count((key == thr) & (kpos < cand)) <= need, cand, bound)

        return lax.fori_loop(0, nbits, tie_body, jnp.zeros((1, tq), jnp.int32))

    excess = jnp.max(count(key >= thr)) > nsel_f
    bound = lax.cond(excess, tie_bound, lambda: jnp.full((1, tq), 1 << nbits, jnp.int32))

    sel = ((key > thr) | ((key == thr) & (kpos < bound))) & valid
    selbias = jnp.where(sel, 0.0, NEG)

    k = k_ref[0]
    vt = vt_ref[0]

    def qk(h):
        return _dot(k, qt_ref[0, h * DH_B:(h + 1) * DH_B, :])

    s_next = qk(0)
    for h in range(H_B):
        s = s_next + selbias
        if h + 1 < H_B:
            s_next = qk(h + 1)
        m = jnp.max(s, axis=0, keepdims=True)
        p = jnp.exp2(s - m)
        l = jnp.sum(p, axis=0, keepdims=True)
        ot = _dot(vt, p.astype(BF16)) / l
        o_ref[0, :, h * DH_B:(h + 1) * DH_B] = ot.T.astype(BF16)


def _dsa_attention(qt, qit, wit, k, vt, ki, *, l_valid, q_off, n_sel, groups):
    b, nq, nf, tq = qt.shape
    lpad = k.shape[1]
    nqg = nq // groups
    tg = nqg * tq
    outs = []
    for gi in range(groups):
        q_end = q_off + (gi + 1) * tg - 1
        k_end = min(l_valid, ((q_end >> CHUNK_SHIFT) + 1) << CHUNK_SHIFT)
        lk = min(lpad, _round_up(k_end, LANES))
        qmap = lambda bi, i, gi=gi: (bi, gi * nqg + i, 0, 0)
        kmap = lambda bi, i: (bi, 0, 0)
        outs.append(pl.pallas_call(
            functools.partial(_dsa_kernel, tq=tq, lk=lk, l_valid=l_valid, q_off=q_off + gi * tg, n_sel=n_sel),
            grid=(b, nqg),
            in_specs=[
                _tiled_spec(nf, tq, qmap),
                _tiled_spec(qit.shape[2], tq, qmap),
                _tiled_spec(wit.shape[2], tq, qmap),
                pl.BlockSpec((1, lk, DH_B), kmap),
                pl.BlockSpec((1, DH_B, lk), kmap),
                pl.BlockSpec((1, lk, DH_IDX), kmap),
            ],
            out_specs=pl.BlockSpec((1, tq, nf), lambda bi, i: (bi, i, 0)),
            out_shape=jax.ShapeDtypeStruct((b, tg, nf), BF16),
            compiler_params=_cparams(("parallel", "parallel")),
            name="dsa_attention",
        )(qt, qit, wit, k, vt, ki))
    return outs[0] if groups == 1 else jnp.concatenate(outs, axis=1)


def _proj_c_kernel(x_ref, g_ref, wcq_ref, wckv_ref, wkpe_ref, gq_ref, gkv_ref, wuqt_ref,
                   ck_ref, sk_ref, ct_ref, st_ref, qt_ref, ckv_ref, kpe_ref):
    h = _rms(x_ref[...], g_ref[...]).astype(BF16)
    cq = _rms(_dot(h, wcq_ref[...]), gq_ref[...]).astype(BF16)
    ckv_ref[...] = _rms(_dot(h, wckv_ref[...]), gkv_ref[...])
    kpe_ref[...] = _rope_slab(_dot(h, wkpe_ref[...]), ck_ref[...], sk_ref[...], QK_ROPE)
    qt = _dot_nt(wuqt_ref[...], cq)
    cos, sin = ct_ref[...], st_ref[...]
    scale = (QK_NOPE + QK_ROPE) ** -0.5 * LOG2E
    half = QK_ROPE // 2
    for hh in range(H_C):
        base = hh * LANES
        r0 = base + QK_NOPE
        o1, o2 = _rope_rows(qt[r0:r0 + half, :], qt[r0 + half:r0 + QK_ROPE, :], cos, sin)
        qt_ref[0, base:r0, :] = (qt[base:r0, :] * scale).astype(BF16)
        qt_ref[0, r0:r0 + QK_ROPE, :] = (jnp.concatenate([o1, o2], axis=0) * scale).astype(BF16)
        qt_ref[0, r0 + QK_ROPE:base + LANES, :] = jnp.zeros((LANES - QK_NOPE - QK_ROPE, qt.shape[1]), BF16)


def _proj_c(x, g, w, gq, gkv, wuqt, row_tabs, time_tabs, b, t, tm=256):
    m, d = x.shape
    tm = min(tm, t)
    nt = t // tm
    in_specs = [pl.BlockSpec((tm, d), _row), _resident((1, d))]
    in_specs += [_resident(a.shape) for a in w]
    in_specs += [_resident((1, Q_LORA)), _resident((1, KV_LORA)), _resident(wuqt.shape)]
    in_specs += [pl.BlockSpec((tm, LANES), _row_tab(nt)) for _ in row_tabs]
    in_specs += [pl.BlockSpec((tb.shape[0], tm), _time_tab(nt)) for tb in time_tabs]
    nq = H_C * LANES
    return pl.pallas_call(
        _proj_c_kernel,
        grid=(m // tm,),
        in_specs=in_specs,
        out_specs=[_tiled_spec(nq, tm, _tiled_out(nt)), pl.BlockSpec((tm, KV_LORA), _row),
                   pl.BlockSpec((tm, LANES), _row)],
        out_shape=[jax.ShapeDtypeStruct((b, nt, nq, tm), BF16), jax.ShapeDtypeStruct((m, KV_LORA), F32),
                   jax.ShapeDtypeStruct((m, LANES), F32)],
        compiler_params=_cparams(("parallel",)),
        name="proj_c",
    )(x, g.reshape(1, d), *w, gq.reshape(1, Q_LORA), gkv.reshape(1, KV_LORA), wuqt, *row_tabs, *time_tabs)


def _kv_expand_kernel(ckv_ref, kpe_ref, wuk_ref, wuvt_ref, place_ref, k_ref, vt_ref):
    ckv = ckv_ref[0].astype(BF16)
    kpe = _dot(kpe_ref[0].astype(BF16), place_ref[...])
    kn = _dot(ckv, wuk_ref[...])
    for hh in range(H_C):
        sl = slice(hh * LANES, (hh + 1) * LANES)
        k_ref[0, :, sl] = (kn[:, sl] + kpe).astype(BF16)
    _store_aug_values(vt_ref, _dot_nt(wuvt_ref[...], ckv).astype(BF16))


def _kv_expand(ckv, kpe, wuk, wuvt, place, tm):
    b, l, _ = ckv.shape
    rmap = lambda bi, i: (bi, i, 0)
    cmap = lambda bi, i: (0, 0)
    return pl.pallas_call(
        _kv_expand_kernel,
        grid=(b, l // tm),
        in_specs=[pl.BlockSpec((1, tm, KV_LORA), rmap), pl.BlockSpec((1, tm, LANES), rmap),
                  pl.BlockSpec(wuk.shape, cmap), pl.BlockSpec(wuvt.shape, cmap),
                  pl.BlockSpec(place.shape, cmap)],
        out_specs=[pl.BlockSpec((1, tm, H_C * LANES), rmap),
                   _tiled_spec(H_C * VROWS, tm, lambda bi, i: (bi, i, 0, 0))],
        out_shape=[jax.ShapeDtypeStruct((b, l, H_C * LANES), BF16),
                   jax.ShapeDtypeStruct((b, l // tm, H_C * VROWS, tm), BF16)],
        compiler_params=_cparams(("parallel", "parallel")),
        name="kv_expand",
    )(ckv, kpe, wuk, wuvt, place)


def _rope_angles(pos, dim):
    half = dim // 2
    inv = ROPE_THETA ** (-jnp.arange(half, dtype=F32) / half)
    ang = pos.astype(F32)[:, None] * inv[None, :]
    return jnp.cos(ang), jnp.sin(ang)


def _rope_row_tables(pos, dim):
    cos, sin = _rope_angles(pos, dim)
    reps = LANES // dim
    return (jnp.tile(jnp.concatenate([cos, cos], axis=-1), (1, reps)),
            jnp.tile(jnp.concatenate([-sin, sin], axis=-1), (1, reps)))


def _rope_time_tables(pos, dim):
    cos, sin = _rope_angles(pos, dim)
    return cos.T, sin.T


def _prep_weights(p):
    w = {}
    w["ffn1_in"] = p["ffn1_w_in"].astype(BF16)
    w["ffn1_out"] = p["ffn1_w_out"].astype(BF16)
    w["ffn2_in"] = p["ffn2_w_in"].astype(BF16)
    w["ffn2_out"] = p["ffn2_w_out"].astype(BF16)
    a = p["a_w_in"].astype(BF16)
    n = H_A * DH_A
    w["a"] = (a[:, :n].T, a[:, n:2 * n], a[:, 2 * n:3 * n], a[:, 2 * n:3 * n].T, a[:, 3 * n:])
    w["a_out"] = p["a_w_out"].astype(BF16)
    bw = p["b_w_in"].astype(BF16)
    offs = np.cumsum([0, H_B * DH_B, DH_B, DH_B, H_IDX * DH_IDX, DH_IDX, H_IDX]).tolist()
    bq, bk, bv, bqi, bki, bwi = [bw[:, offs[i]:offs[i + 1]] for i in range(6)]
    w["b"] = (bq.T, bk, bv, bv.T, bqi.T, jnp.concatenate([bki, bki], axis=1), bwi.T)
    w["b_out"] = p["b_w_out"].astype(BF16)
    c = p["c_w_down"].astype(BF16)
    d = c.shape[0]
    wkpe = jnp.zeros((d, LANES), BF16).at[:, :QK_ROPE].set(c[:, Q_LORA + KV_LORA:])
    w["c"] = (c[:, :Q_LORA], c[:, Q_LORA:Q_LORA + KV_LORA], wkpe)
    uq = p["c_w_uq"].astype(BF16).reshape(Q_LORA, H_C, QK_NOPE + QK_ROPE)
    uq = jnp.pad(uq, ((0, 0), (0, 0), (0, LANES - QK_NOPE - QK_ROPE)))
    w["c_uqt"] = uq.reshape(Q_LORA, H_C * LANES).T
    ukv = p["c_w_ukv"].astype(BF16).reshape(KV_LORA, H_C, QK_NOPE + V_C)
    uk = jnp.pad(ukv[:, :, :QK_NOPE], ((0, 0), (0, 0), (0, LANES - QK_NOPE)))
    w["c_uk"] = uk.reshape(KV_LORA, H_C * LANES)
    w["c_uvt"] = ukv[:, :, QK_NOPE:].reshape(KV_LORA, H_C * V_C).T
    w["c_place"] = jnp.zeros((LANES, LANES), BF16).at[jnp.arange(QK_ROPE), QK_NOPE + jnp.arange(QK_ROPE)].set(1.0)
    w["c_out"] = p["c_w_out"].astype(BF16)
    dw = p["d_w_in"].astype(BF16)
    n = H_D * DH_D
    w["d"] = (dw[:, :n].T, dw[:, n:2 * n], dw[:, 2 * n:], dw[:, 2 * n:].T)
    w["d_out"] = p["d_w_out"].astype(BF16)
    return w


def _round_up(x, m):
    return (x + m - 1) // m * m


def _cat_time(past, new, lpad, axis):
    a = new if past is None else jnp.concatenate([past.astype(new.dtype), new], axis=axis)
    if a.shape[axis] < lpad:
        pad = [(0, 0)] * a.ndim
        pad[axis] = (0, lpad - a.shape[axis])
        a = jnp.pad(a, pad)
    return a


def _trunk(x3, past, p, w):
    b, t, d = x3.shape
    m = b * t
    x = x3.reshape(m, d)
    past_len = 0 if past is None else past[0][0].shape[1]
    l_valid = past_len + t
    pos = past_len + jnp.arange(t, dtype=jnp.int32)
    if past is None:
        tile = min(256, t)
        tile_s = min(512, t)
        lpad = _round_up(l_valid, tile_s)
        ktile, ktile_s, tpad = tile, tile_s, t
        groups = 4 if t % (4 * tile) == 0 else 1
    else:
        tile = tile_s = tpad = _round_up(t, LANES)
        lpad = ktile_s = _round_up(l_valid, LANES)
        ktile = LANES
        groups = 1
    norm_g = p["norm_g"]
    rows = {}

    def r3(a):
        return a.reshape(b, t, a.shape[-1])

    def q_in(a4):
        return a4 if tpad == t else jnp.pad(a4, ((0, 0), (0, 0), (0, 0), (0, tpad - t)))

    def v_in(vt4, past_t, kt):
        if past is None:
            return vt4
        a = _cat_time(past_t, vt4.reshape(b, vt4.shape[2], t), lpad, 2)
        return jnp.transpose(a.reshape(b, a.shape[1], lpad // kt, kt), (0, 2, 1, 3))

    def unpad(o):
        return (o if tpad == t else o[:, :t]).reshape(m, -1)

    def pst(idx, j, width):
        return None if past is None else past[idx][j].reshape(b, past_len, width)

    def pst_t(idx, j, width):
        return None if past is None else jnp.transpose(past[idx][j].reshape(b, past_len, width), (0, 2, 1))

    def aug(vt_past):
        if vt_past is None:
            return None
        dv = LANES // 2
        v4 = vt_past.reshape(b, -1, dv, past_len)
        extra = jnp.zeros((b, v4.shape[1], VROWS - dv, past_len), v4.dtype).at[:, :, 0, :].set(1.0)
        return jnp.concatenate([v4, extra], axis=2).reshape(b, -1, past_len)

    for i in range(4):
        x = _ffn(x, norm_g[i, 0], w["ffn1_in"][i], w["ffn1_out"][i])
        g = norm_g[i, 1]
        if i == 0:
            wqt, wk, wv, wvt, wf = w["a"]
            qt, k, kb, v, vt, lf = _proj_qkv(x, g, wqt, wk, wv, wvt, DH_A ** -0.5 * LOG2E, b, t, wf, p["a_b_f"],
                                             tm=tile_s)
            rows["a"] = (k.reshape(b, t, H_A, DH_A), v.reshape(b, t, H_A, DH_A), lf.reshape(b, t, H_A))
            lf_all = _cat_time(None if past is None else past[0][2].astype(F32), r3(lf), lpad, 1)
            c = _cumsum_time(jnp.pad(lf_all, ((0, 0), (0, 0), (0, LANES - H_A))))
            o = _attention(q_in(qt), _cat_time(pst(0, 0, H_A * DH_A), r3(kb), lpad, 1),
                           v_in(vt, aug(pst_t(0, 1, H_A * DH_A)), ktile_s), c,
                           nh=H_A, k_per_head=False, chunked=False, q_off=past_len, l_valid=l_valid)
            mix = (unpad(o), w["a_out"])
        elif i == 1:
            row_tabs = list(_rope_row_tables(pos, DH_B)) + list(_rope_row_tables(pos, DH_IDX))
            time_tabs = list(_rope_time_tables(pos, DH_B)) + list(_rope_time_tables(pos, DH_IDX))
            qt, k, kb, v, vt, qit, ki, kib, wit = _proj_b(x, g, w["b"], row_tabs, time_tabs, b, t, tm=tile)
            rows["b"] = (r3(k), r3(v), r3(ki))
            n_sel = min(IDX_TOPK, l_valid // 4)
            o = _dsa_attention(q_in(qt), q_in(qit), q_in(wit), _cat_time(pst(1, 0, DH_B), r3(kb), lpad, 1),
                               _cat_time(pst_t(1, 1, DH_B), vt, lpad, 2),
                               _cat_time(pst(1, 2, DH_IDX), r3(kib), lpad, 1),
                               l_valid=l_valid, q_off=past_len, n_sel=n_sel, groups=groups)
            mix = (unpad(o), w["b_out"])
        elif i == 2:
            row_tabs = list(_rope_row_tables(pos, QK_ROPE))
            time_tabs = list(_rope_time_tables(pos, QK_ROPE))
            qt, ckv, kpe = _proj_c(x, g, w["c"], p["c_g_q"], p["c_g_kv"], w["c_uqt"], row_tabs, time_tabs, b, t,
                                   tm=tile_s)
            rows["c"] = (r3(ckv), r3(kpe)[:, :, :QK_ROPE])
            kpe_past = None if past is None else jnp.pad(past[2][1].astype(F32),
                                                         ((0, 0), (0, 0), (0, LANES - QK_ROPE)))
            ckv_all = _cat_time(None if past is None else past[2][0].astype(F32), r3(ckv), lpad, 1)
            kpe_all = _cat_time(kpe_past, r3(kpe), lpad, 1)
            kc, vct = _kv_expand(ckv_all, kpe_all, w["c_uk"], w["c_uvt"], w["c_place"], tm=ktile_s)
            o = _attention(q_in(qt), kc, vct, None,
                           nh=H_C, k_per_head=True, chunked=True, q_off=past_len, l_valid=l_valid)
            mix = (unpad(o), w["c_out"])
        else:
            wqt, wk, wv, wvt = w["d"]
            qt, k, kb, v, vt = _proj_qkv(x, g, wqt, wk, wv, wvt, DH_D ** -0.5 * LOG2E, b, t, tm=tile)
            rows["d"] = (k.reshape(b, t, H_D, DH_D), v.reshape(b, t, H_D, DH_D))
            o = _sb_attention(q_in(qt), _cat_time(pst(3, 0, H_D * DH_D), r3(kb), lpad, 1),
                              v_in(vt, pst_t(3, 1, H_D * DH_D), ktile), nh=H_D, q_off=past_len)
            mix = (unpad(o), w["d_out"])
        x = _ffn(x, norm_g[i, 2], w["ffn2_in"][i], w["ffn2_out"][i], mix, p["final_g"] if i == 3 else None)
    return x.reshape(b, t, d), rows


def kernel(x_prompt, x_sample, cache_a_k, cache_a_v, cache_a_logf, cache_b_k, cache_b_v, cache_b_kidx,
           cache_c_ckv, cache_c_kpe, cache_d_k, cache_d_v, norm_g, final_g,
           ffn1_w_in, ffn1_w_out, ffn2_w_in, ffn2_w_out, a_w_in, a_b_f, a_w_out, b_w_in, b_w_out,
           c_w_down, c_g_q, c_g_kv, c_w_uq, c_w_ukv, c_w_out, d_w_in, d_w_out):
    p = dict(norm_g=norm_g, final_g=final_g, ffn1_w_in=ffn1_w_in, ffn1_w_out=ffn1_w_out,
             ffn2_w_in=ffn2_w_in, ffn2_w_out=ffn2_w_out, a_w_in=a_w_in, a_b_f=a_b_f, a_w_out=a_w_out,
             b_w_in=b_w_in, b_w_out=b_w_out, c_w_down=c_w_down, c_g_q=c_g_q, c_g_kv=c_g_kv,
             c_w_uq=c_w_uq, c_w_ukv=c_w_ukv, c_w_out=c_w_out, d_w_in=d_w_in, d_w_out=d_w_out)
    w = _prep_weights(p)
    y_prompt, pr = _trunk(x_prompt, None, p, w)
    past = ((cache_a_k, cache_a_v, cache_a_logf), (cache_b_k, cache_b_v, cache_b_kidx),
            (cache_c_ckv, cache_c_kpe), (cache_d_k, cache_d_v))
    y_sample, sr = _trunk(x_sample, past, p, w)
    return (y_prompt, y_sample, *pr["a"], *pr["b"], *pr["c"], *pr["d"],
            *sr["a"], *sr["b"], *sr["c"], *sr["d"])
```

```python
import functools

import jax
import jax.numpy as jnp
import numpy as np
from jax import lax
from jax.experimental import pallas as pl
from jax.experimental.pallas import tpu as pltpu

F32 = jnp.float32
BF16 = jnp.bfloat16

D_MODEL = 1024
CHUNK_SHIFT = 6
ROPE_THETA = 10000.0
RMS_EPS = 1e-6
D_FF = 2816
H_A, DH_A = 16, 64
H_B, DH_B = 8, 128
H_IDX, DH_IDX = 8, 64
IDX_TOPK = 256
H_C, Q_LORA, KV_LORA, QK_NOPE, QK_ROPE, V_C = 16, 256, 128, 64, 32, 64
H_D, DH_D = 16, 64

LANES = 128
NEG = -1e30
LOG2E = 1.4426950408889634
EXP2_ZERO = -150.0
VROWS = 80
VMEM_LIMIT = 56 * 1024 * 1024


def _cparams(sem):
    return pltpu.CompilerParams(dimension_semantics=sem, vmem_limit_bytes=VMEM_LIMIT)


def _dot(a, b):
    return jnp.dot(a, b, preferred_element_type=F32)


def _dot_nt(a, b):
    return lax.dot_general(a, b, (((1,), (1,)), ((), ())), preferred_element_type=F32)


def _rms(x, g):
    return x * lax.rsqrt(jnp.mean(x * x, axis=-1, keepdims=True) + RMS_EPS) * g


def _softplus_neg_abs(z):
    return jnp.log(1.0 + jnp.exp(-jnp.abs(z)))


def _log_sigmoid(z):
    return jnp.minimum(z, 0.0) - _softplus_neg_abs(z)


def _rope_slab(x, cos, sin, group):
    half = group // 2
    if group == LANES:
        partner = pltpu.roll(x, half, 1)
    else:
        lane = lax.broadcasted_iota(jnp.int32, (1, LANES), 1)
        first = (lane & (group - 1)) < half
        partner = jnp.where(first, pltpu.roll(x, LANES - half, 1), pltpu.roll(x, half, 1))
    return x * cos + partner * sin


def _store_aug_values(vt_ref, vt):
    dv = LANES // 2
    n = vt.shape[1]
    extra = jnp.where(lax.broadcasted_iota(jnp.int32, (VROWS - dv, n), 0) == 0, 1.0, 0.0).astype(BF16)
    for hh in range(vt.shape[0] // dv):
        vt_ref[0, hh * VROWS:hh * VROWS + dv, :] = vt[hh * dv:(hh + 1) * dv, :]
        vt_ref[0, hh * VROWS + dv:(hh + 1) * VROWS, :] = extra


def _rope_rows(x1, x2, cos, sin):
    return x1 * cos - x2 * sin, x1 * sin + x2 * cos


FFN_CHUNK = 256


def _ffn_kernel(x_ref, *rest, mix, final):
    rest = list(rest)
    if mix:
        o_ref, wmix_ref = rest.pop(0), rest.pop(0)
    g_ref, win_ref, wo_ref = rest.pop(0), rest.pop(0), rest.pop(0)
    if final:
        fg_ref = rest.pop(0)
    (y_ref,) = rest
    dff = wo_ref.shape[0]
    x = x_ref[...]
    if mix:
        x = x + _dot(o_ref[...], wmix_ref[...])
    h = _rms(x, g_ref[...]).astype(BF16)

    def inproj(c):
        lo = c * FFN_CHUNK
        return _dot(h, win_ref[:, lo:lo + FFN_CHUNK]), _dot(h, win_ref[:, dff + lo:dff + lo + FFN_CHUNK])

    acc = None
    nxt = inproj(0)
    for c in range(dff // FFN_CHUNK):
        up, gate = nxt
        if (c + 1) * FFN_CHUNK < dff:
            nxt = inproj(c + 1)
        act = (gate / (1.0 + jnp.exp(-gate)) * up).astype(BF16)
        part = _dot(act, wo_ref[c * FFN_CHUNK:(c + 1) * FFN_CHUNK, :])
        acc = part if acc is None else acc + part
    y = x + 0.5 * acc
    if final:
        y = _rms(y, fg_ref[...])
    y_ref[...] = y


def _resident(shape):
    return pl.BlockSpec(shape, lambda i: (0,) * len(shape), pipeline_mode=pl.Buffered(1))


def _ffn(x, g, w_in, w_out, mix=None, final_g=None, tm=512):
    m, d = x.shape
    dff = w_out.shape[0]
    tm = min(tm, m)
    row = pl.BlockSpec((tm, d), lambda i: (i, 0))
    in_specs, args = [row], [x]
    if mix is not None:
        o, wmix = mix
        in_specs += [pl.BlockSpec((tm, o.shape[1]), lambda i: (i, 0)), _resident(wmix.shape)]
        args += [o, wmix]
    in_specs += [_resident((1, d)), _resident((d, 2 * dff)), _resident((dff, d))]
    args += [g.reshape(1, d), w_in, w_out]
    if final_g is not None:
        in_specs.append(_resident((1, d)))
        args.append(final_g.reshape(1, d))
    return pl.pallas_call(
        functools.partial(_ffn_kernel, mix=mix is not None, final=final_g is not None),
        grid=(m // tm,),
        in_specs=in_specs,
        out_specs=row,
        out_shape=jax.ShapeDtypeStruct((m, d), F32),
        compiler_params=_cparams(("parallel",)),
        name="ffn",
    )(*args)


def _row(i):
    return (i, 0)


def _feat_time(nt):
    return lambda i: (i // nt, 0, i % nt)


def _tiled_spec(feat, tile, index_map):
    return pl.BlockSpec((None, 1, feat, tile), index_map)


def _tiled_out(nt):
    return lambda i: (i // nt, i % nt, 0, 0)


def _time_tab(nt):
    return lambda i: (0, i % nt)


def _row_tab(nt):
    return lambda i: (i % nt, 0)


def _proj_qkv_kernel(x_ref, g_ref, wqt_ref, wk_ref, wv_ref, wvt_ref, *rest, scale, has_f):
    if has_f:
        wf_ref, bf_ref, qt_ref, k_ref, kb_ref, v_ref, vt_ref, lf_ref = rest
    else:
        qt_ref, k_ref, kb_ref, v_ref, vt_ref = rest
    h = _rms(x_ref[...], g_ref[...]).astype(BF16)
    qt = (_dot_nt(wqt_ref[...], h) * scale).astype(BF16)
    half = LANES // 2
    zeros = jnp.zeros((half, qt.shape[1]), BF16)
    for hh in range(qt.shape[0] // half):
        base = hh * LANES
        own = slice(base + (hh % 2) * half, base + (hh % 2) * half + half)
        other = slice(base + (1 - hh % 2) * half, base + (1 - hh % 2) * half + half)
        qt_ref[0, own, :] = qt[hh * half:(hh + 1) * half, :]
        qt_ref[0, other, :] = zeros
    k = _dot(h, wk_ref[...])
    k_ref[...] = k.reshape(k_ref.shape)
    kb_ref[...] = k.astype(BF16)
    v_ref[...] = _dot(h, wv_ref[...]).reshape(v_ref.shape)
    vt = _dot_nt(wvt_ref[...], h).astype(BF16)
    if has_f:
        _store_aug_values(vt_ref, vt)
        lf_ref[...] = _log_sigmoid(_dot(h, wf_ref[...]) + bf_ref[...])
    else:
        vt_ref[0] = vt


def _proj_qkv(x, g, wqt, wk, wv, wvt, scale, b, t, wf=None, bf=None, tm=256):
    m, d = x.shape
    n = wk.shape[1]
    tm = min(tm, t)
    nt = t // tm
    has_f = wf is not None
    nvt = n // (LANES // 2) * VROWS if has_f else n
    in_specs = [pl.BlockSpec((tm, d), _row), _resident((1, d))] + [_resident((d, n))] * 4
    args = [x, g.reshape(1, d), wqt, wk, wv, wvt]
    dh = LANES // 2
    heads = pl.BlockSpec((tm, n // dh, dh), lambda i: (i, 0, 0))
    out_specs = [_tiled_spec(2 * n, tm, _tiled_out(nt)), heads,
                 pl.BlockSpec((tm, n), _row), heads, _tiled_spec(nvt, tm, _tiled_out(nt))]
    out_shape = [jax.ShapeDtypeStruct((b, nt, 2 * n, tm), BF16), jax.ShapeDtypeStruct((m, n // dh, dh), F32),
                 jax.ShapeDtypeStruct((m, n), BF16), jax.ShapeDtypeStruct((m, n // dh, dh), F32),
                 jax.ShapeDtypeStruct((b, nt, nvt, tm), BF16)]
    if has_f:
        nh = wf.shape[1]
        in_specs += [_resident((d, nh)), _resident((1, nh))]
        args += [wf, bf.reshape(1, nh)]
        out_specs.append(pl.BlockSpec((tm, nh), _row))
        out_shape.append(jax.ShapeDtypeStruct((m, nh), F32))
    return pl.pallas_call(
        functools.partial(_proj_qkv_kernel, scale=scale, has_f=has_f),
        grid=(m // tm,),
        in_specs=in_specs,
        out_specs=out_specs,
        out_shape=out_shape,
        compiler_params=_cparams(("parallel",)),
        name="proj_qkv",
    )(*args)


def _split3(x):
    hi = x.astype(BF16)
    r1 = x - hi.astype(F32)
    mid = r1.astype(BF16)
    lo = (r1 - mid.astype(F32)).astype(BF16)
    return hi, mid, lo


def _cumsum_kernel(x_ref, c_ref, *, nblk):
    r = lax.broadcasted_iota(jnp.int32, (LANES, LANES), 0)
    c = lax.broadcasted_iota(jnp.int32, (LANES, LANES), 1)
    tri = jnp.where(c <= r, 1.0, 0.0).astype(BF16)
    carry = jnp.zeros((1, x_ref.shape[2]), F32)
    for b in range(nblk):
        hi, mid, lo = _split3(x_ref[0, b * LANES:(b + 1) * LANES, :])
        blk = (_dot(tri, hi) + _dot(tri, mid)) + _dot(tri, lo) + carry
        c_ref[0, b * LANES:(b + 1) * LANES, :] = blk
        carry = blk[LANES - 1:LANES, :]


def _cumsum_time(x):
    b, l, nh = x.shape
    spec = pl.BlockSpec((1, l, nh), lambda i: (i, 0, 0))
    return pl.pallas_call(
        functools.partial(_cumsum_kernel, nblk=l // LANES),
        grid=(b,),
        in_specs=[spec],
        out_specs=spec,
        out_shape=jax.ShapeDtypeStruct((b, l, nh), F32),
        compiler_params=_cparams(("parallel",)),
        name="cumsum_time",
    )(x)


def _last_block(i, tq, tk, q_off, chunked, nk):
    q_last = q_off + (i + 1) * tq - 1
    if chunked:
        q_last = ((q_last >> CHUNK_SHIFT) << CHUNK_SHIFT) + (1 << CHUNK_SHIFT) - 1
    return jnp.minimum(q_last // tk, nk - 1)


def _block_pairs(nq, tq, tk, q_off, chunked, nk, reverse):
    ii, jj = [], []
    for i in range(nq):
        q_last = q_off + (i + 1) * tq - 1
        if chunked:
            q_last = ((q_last >> CHUNK_SHIFT) << CHUNK_SHIFT) + (1 << CHUNK_SHIFT) - 1
        blocks = list(range(min(q_last // tk, nk - 1) + 1))
        for j in (reversed(blocks) if reverse else blocks):
            ii.append(i)
            jj.append(j)
    return jnp.asarray(ii, jnp.int32), jnp.asarray(jj, jnp.int32)


def _attn_kernel(ii_ref, jj_ref, qt_ref, k_ref, vt_ref, *rest, nh, k_per_head, bias, chunked, tq, tk, q_off, l_valid,
                 nk):
    if bias:
        c_ref, o_ref, m_ref, acc_ref = rest
    else:
        o_ref, m_ref, acc_ref = rest
    i = ii_ref[pl.program_id(1)]
    j = jj_ref[pl.program_id(1)]
    dv = LANES // 2

    @pl.when(j == 0)
    def _():
        m_ref[...] = jnp.full_like(m_ref, NEG)
        acc_ref[...] = jnp.zeros_like(acc_ref)

    q_first = q_off + i * tq
    k_last = j * tk + tk - 1
    if chunked:
        full = ((k_last >> CHUNK_SHIFT) <= (q_first >> CHUNK_SHIFT)) & (k_last < l_valid)
    else:
        full = k_last <= q_first

    def step(masked):
        if masked:
            kpos = j * tk + lax.broadcasted_iota(jnp.int32, (tk, tq), 0)
            qpos = q_first + lax.broadcasted_iota(jnp.int32, (tk, tq), 1)
            if chunked:
                valid = ((kpos >> CHUNK_SHIFT) <= (qpos >> CHUNK_SHIFT)) & (kpos < l_valid)
            else:
                valid = kpos <= qpos
            maskbias = jnp.where(valid, 0.0, NEG)
        def qk(h):
            ks = h if k_per_head else h // 2
            return _dot(k_ref[0, :, ks * LANES:(ks + 1) * LANES], qt_ref[0, h * LANES:(h + 1) * LANES, :])

        s_next = qk(0)
        for h in range(nh):
            s = s_next
            if h + 1 < nh:
                s_next = qk(h + 1)
            if bias:
                s = s - c_ref[0, :, h:h + 1] * LOG2E
            if masked:
                s = s + maskbias
            m_prev = m_ref[h:h + 1, :]
            m_new = jnp.maximum(m_prev, jnp.max(s, axis=0, keepdims=True))
            alpha = jnp.exp2(m_prev - m_new)
            p = jnp.exp2(s - m_new)
            m_ref[h:h + 1, :] = m_new
            rows = slice(h * VROWS, (h + 1) * VROWS)
            acc_ref[rows, :] = alpha * acc_ref[rows, :] + _dot(vt_ref[0, rows, :], p.astype(BF16))

    @pl.when(full)
    def _():
        step(False)

    @pl.when(jnp.logical_not(full))
    def _():
        step(True)

    @pl.when(j == _last_block(i, tq, tk, q_off, chunked, nk))
    def _():
        for p in range(nh // 2):
            parts = [acc_ref[h * VROWS:h * VROWS + dv, :] / acc_ref[h * VROWS + dv:h * VROWS + dv + 1, :]
                     for h in (2 * p, 2 * p + 1)]
            o_ref[0, :, p * LANES:(p + 1) * LANES] = jnp.concatenate(parts, axis=0).T.astype(BF16)


def _attention(qt, k, vt, c, *, nh, k_per_head, chunked, q_off, l_valid):
    b, nq, _, tq = qt.shape
    nk, tk = vt.shape[1], vt.shape[3]
    assert k.shape[1] == nk * tk
    bias = c is not None
    ii, jj = _block_pairs(nq, tq, tk, q_off, chunked, nk, reverse=False)
    in_specs = [
        _tiled_spec(qt.shape[2], tq, lambda bi, p, ii, jj: (bi, ii[p], 0, 0)),
        pl.BlockSpec((1, tk, k.shape[2]), lambda bi, p, ii, jj: (bi, jj[p], 0)),
        _tiled_spec(vt.shape[2], tk, lambda bi, p, ii, jj: (bi, jj[p], 0, 0)),
    ]
    args = [qt, k, vt]
    if bias:
        in_specs.append(pl.BlockSpec((1, tk, c.shape[2]), lambda bi, p, ii, jj: (bi, jj[p], 0)))
        args.append(c)
    dv = nh * (LANES // 2)
    return pl.pallas_call(
        functools.partial(_attn_kernel, nh=nh, k_per_head=k_per_head, bias=bias, chunked=chunked,
                          tq=tq, tk=tk, q_off=q_off, l_valid=l_valid, nk=nk),
        grid_spec=pltpu.PrefetchScalarGridSpec(
            num_scalar_prefetch=2,
            grid=(b, int(ii.shape[0])),
            in_specs=in_specs,
            out_specs=pl.BlockSpec((1, tq, dv), lambda bi, p, ii, jj: (bi, ii[p], 0)),
            scratch_shapes=[pltpu.VMEM((nh, tq), F32), pltpu.VMEM((vt.shape[2], tq), F32)]),
        out_shape=jax.ShapeDtypeStruct((b, nq * tq, dv), BF16),
        compiler_params=_cparams(("parallel", "arbitrary")),
        name="attention",
    )(ii, jj, *args)


def _sb_kernel(ii_ref, jj_ref, qt_ref, k_ref, vt_ref, o_ref, tail_ref, acc_ref, *, nh, tq, tk, q_off, nk):
    i = ii_ref[pl.program_id(1)]
    jj = jj_ref[pl.program_id(1)]
    dv = LANES // 2

    @pl.when(jj == _last_block(i, tq, tk, q_off, False, nk))
    def _():
        tail_ref[...] = jnp.zeros_like(tail_ref)
        acc_ref[...] = jnp.zeros_like(acc_ref)

    q_first = q_off + i * tq
    full = jj * tk + tk - 1 < q_first
    active = jnp.max(tail_ref[...]) > EXP2_ZERO

    def step(masked):
        r = lax.broadcasted_iota(jnp.int32, (tk + 16, tk), 0)
        c = lax.broadcasted_iota(jnp.int32, (tk + 16, tk), 1)
        after = jnp.where(((r < tk) & (c > r)) | (r == tk), 1.0, 0.0).astype(BF16)
        if masked:
            kpos = jj * tk + lax.broadcasted_iota(jnp.int32, (tk, tq), 0)
            qpos = q_first + lax.broadcasted_iota(jnp.int32, (tk, tq), 1)
            valid = kpos < qpos

        def qk(h):
            return _dot(k_ref[0, :, (h // 2) * LANES:(h // 2 + 1) * LANES], qt_ref[0, h * LANES:(h + 1) * LANES, :])

        def logs(z):
            log_beta = jnp.minimum(z, 0.0) - jnp.log2(1.0 + jnp.exp2(-jnp.abs(z)))
            log_1mb = log_beta - z
            if masked:
                log_1mb = jnp.where(valid, log_1mb, 0.0)
            return log_beta, _dot(after, log_1mb.astype(BF16))

        def finish(h, log_beta, sums):
            t_prev = tail_ref[h:h + 1, :]
            a = jnp.exp2(log_beta + sums[:tk, :] + t_prev)
            if masked:
                a = jnp.where(valid, a, 0.0)
            tail_ref[h:h + 1, :] = t_prev + sums[tk:tk + 1, :]
            rows = slice(h * dv, (h + 1) * dv)
            acc_ref[rows, :] += _dot(vt_ref[0, rows, :], a.astype(BF16))

        z_next = qk(0)
        pending = None
        for h in range(nh):
            z = z_next
            if h + 1 < nh:
                z_next = qk(h + 1)
            cur = logs(z)
            if pending is not None:
                finish(h - 1, *pending)
            pending = cur
        finish(nh - 1, *pending)

    @pl.when(active & full)
    def _():
        step(False)

    @pl.when(active & jnp.logical_not(full))
    def _():
        step(True)

    @pl.when(jj == 0)
    def _():
        for p in range(nh // 2):
            o_ref[0, :, p * LANES:(p + 1) * LANES] = acc_ref[p * LANES:(p + 1) * LANES, :].T.astype(BF16)


def _sb_attention(qt, k, vt, *, nh, q_off):
    b, nq, _, tq = qt.shape
    nk, dv, tk = vt.shape[1:]
    assert k.shape[1] == nk * tk

    ii, jj = _block_pairs(nq, tq, tk, q_off, False, nk, reverse=True)
    return pl.pallas_call(
        functools.partial(_sb_kernel, nh=nh, tq=tq, tk=tk, q_off=q_off, nk=nk),
        grid_spec=pltpu.PrefetchScalarGridSpec(
            num_scalar_prefetch=2,
            grid=(b, int(ii.shape[0])),
            in_specs=[
                _tiled_spec(qt.shape[2], tq, lambda bi, p, ii, jj: (bi, ii[p], 0, 0)),
                pl.BlockSpec((1, tk, k.shape[2]), lambda bi, p, ii, jj: (bi, jj[p], 0)),
                _tiled_spec(dv, tk, lambda bi, p, ii, jj: (bi, jj[p], 0, 0)),
            ],
            out_specs=pl.BlockSpec((1, tq, dv), lambda bi, p, ii, jj: (bi, ii[p], 0)),
            scratch_shapes=[pltpu.VMEM((nh, tq), F32), pltpu.VMEM((dv, tq), F32)]),
        out_shape=jax.ShapeDtypeStruct((b, nq * tq, dv), BF16),
        compiler_params=_cparams(("parallel", "arbitrary")),
        name="sb_attention",
    )(ii, jj, qt, k, vt)


def _proj_b_kernel(x_ref, g_ref, wqt_ref, wk_ref, wv_ref, wvt_ref, wqit_ref, wki_ref, wwit_ref,
                   c128_ref, s128_ref, c64_ref, s64_ref, ct64_ref, st64_ref, ct32_ref, st32_ref,
                   qt_ref, k_ref, kb_ref, v_ref, vt_ref, qit_ref, ki_ref, kib_ref, wit_ref):
    h = _rms(x_ref[...], g_ref[...]).astype(BF16)
    qt = _dot_nt(wqt_ref[...], h)
    cos, sin = ct64_ref[...], st64_ref[...]
    half = DH_B // 2
    for hh in range(H_B):
        base = hh * DH_B
        o1, o2 = _rope_rows(qt[base:base + half, :], qt[base + half:base + DH_B, :], cos, sin)
        qt_ref[0, base:base + half, :] = (o1 * (DH_B ** -0.5 * LOG2E)).astype(BF16)
        qt_ref[0, base + half:base + DH_B, :] = (o2 * (DH_B ** -0.5 * LOG2E)).astype(BF16)
    k = _rope_slab(_dot(h, wk_ref[...]), c128_ref[...], s128_ref[...], DH_B)
    k_ref[...] = k
    kb_ref[...] = k.astype(BF16)
    v_ref[...] = _dot(h, wv_ref[...])
    vt_ref[0] = _dot_nt(wvt_ref[...], h).astype(BF16)
    qit = _dot_nt(wqit_ref[...], h)
    cos, sin = ct32_ref[...], st32_ref[...]
    half = DH_IDX // 2
    for hh in range(H_IDX):
        base = hh * DH_IDX
        o1, o2 = _rope_rows(qit[base:base + half, :], qit[base + half:base + DH_IDX, :], cos, sin)
        qit_ref[0, base:base + half, :] = (o1 * (DH_IDX ** -0.5)).astype(BF16)
        qit_ref[0, base + half:base + DH_IDX, :] = (o2 * (DH_IDX ** -0.5)).astype(BF16)
    ki = _rope_slab(_dot(h, wki_ref[...]), c64_ref[...], s64_ref[...], DH_IDX)[:, :DH_IDX]
    ki_ref[...] = ki
    kib_ref[...] = ki.astype(BF16)
    wit_ref[0] = _dot_nt(wwit_ref[...], h) * (H_IDX ** -0.5)


def _proj_b(x, g, w, row_tabs, time_tabs, b, t, tm=256):
    m, d = x.shape
    tm = min(tm, t)
    nt = t // tm
    in_specs = [pl.BlockSpec((tm, d), _row), _resident((1, d))]
    in_specs += [_resident(a.shape) for a in w]
    in_specs += [pl.BlockSpec((tm, LANES), _row_tab(nt)) for _ in row_tabs]
    in_specs += [pl.BlockSpec((tb.shape[0], tm), _time_tab(nt)) for tb in time_tabs]
    nq, ni = H_B * DH_B, H_IDX * DH_IDX
    out_specs = [_tiled_spec(nq, tm, _tiled_out(nt)), pl.BlockSpec((tm, DH_B), _row),
                 pl.BlockSpec((tm, DH_B), _row), pl.BlockSpec((tm, DH_B), _row),
                 pl.BlockSpec((1, DH_B, tm), _feat_time(nt)), _tiled_spec(ni, tm, _tiled_out(nt)),
                 pl.BlockSpec((tm, DH_IDX), _row), pl.BlockSpec((tm, DH_IDX), _row),
                 _tiled_spec(H_IDX, tm, _tiled_out(nt))]
    out_shape = [jax.ShapeDtypeStruct((b, nt, nq, tm), BF16), jax.ShapeDtypeStruct((m, DH_B), F32),
                 jax.ShapeDtypeStruct((m, DH_B), BF16), jax.ShapeDtypeStruct((m, DH_B), F32),
                 jax.ShapeDtypeStruct((b, DH_B, t), BF16), jax.ShapeDtypeStruct((b, nt, ni, tm), BF16),
                 jax.ShapeDtypeStruct((m, DH_IDX), F32), jax.ShapeDtypeStruct((m, DH_IDX), BF16),
                 jax.ShapeDtypeStruct((b, nt, H_IDX, tm), F32)]
    return pl.pallas_call(
        _proj_b_kernel,
        grid=(m // tm,),
        in_specs=in_specs,
        out_specs=out_specs,
        out_shape=out_shape,
        compiler_params=_cparams(("parallel",)),
        name="proj_b",
    )(x, g.reshape(1, d), *w, *row_tabs, *time_tabs)


def _dsa_kernel(qt_ref, qit_ref, wit_ref, k_ref, vt_ref, ki_ref, o_ref, *, tq, lk, l_valid, q_off, n_sel):
    i = pl.program_id(1)
    kpos = lax.broadcasted_iota(jnp.int32, (lk, tq), 0)
    qpos = q_off + i * tq + lax.broadcasted_iota(jnp.int32, (lk, tq), 1)
    valid = ((kpos >> CHUNK_SHIFT) <= (qpos >> CHUNK_SHIFT)) & (kpos < l_valid)

    ki = ki_ref[0]
    score = jnp.zeros((lk, tq), F32)
    for h in range(H_IDX):
        d = _dot(ki, qit_ref[0, h * DH_IDX:(h + 1) * DH_IDX, :])
        score = score + wit_ref[0, h:h + 1, :] * jnp.maximum(d, 0.0)
    score = jnp.where(valid, score + 0.0, -jnp.inf)

    bits = pltpu.bitcast(score, jnp.int32)
    key = jnp.where(bits < 0, bits ^ jnp.int32(0x7FFFFFFF), bits)

    int_min = jnp.int32(-2 ** 31)
    nsel_f = jnp.float32(n_sel)

    def count(pred):
        return jnp.sum(jnp.where(pred, 1.0, 0.0), axis=0, keepdims=True)

    def thr_body(it, t):
        cand = t + lax.shift_left(jnp.int32(1), jnp.int32(31) - it)
        return jnp.where(count(key >= cand) >= nsel_f, cand, t)

    thr = lax.fori_loop(0, 32, thr_body, jnp.full((1, tq), int_min, jnp.int32))

    nbits = int(lk).bit_length()

    def tie_bound():
        need = nsel_f - count(key > thr)

        def tie_body(it, bound):
            cand = bound + lax.shift_left(jnp.int32(1), jnp.int32(nbits - 1) - it)
            return jnp.where(count((key == thr) & (kpos < cand)) <= need, cand, bound)

        return lax.fori_loop(0, nbits, tie_body, jnp.zeros((1, tq), jnp.int32))

    excess = jnp.max(count(key >= thr)) > nsel_f
    bound = lax.cond(excess, tie_bound, lambda: jnp.full((1, tq), 1 << nbits, jnp.int32))

    sel = ((key > thr) | ((key == thr) & (kpos < bound))) & valid
    selbias = jnp.where(sel, 0.0, NEG)

    k = k_ref[0]
    vt = vt_ref[0]

    def qk(h):
        return _dot(k, qt_ref[0, h * DH_B:(h + 1) * DH_B, :])

    s_next = qk(0)
    for h in range(H_B):
        s = s_next + selbias
        if h + 1 < H_B:
            s_next = qk(h + 1)
        m = jnp.max(s, axis=0, keepdims=True)
        p = jnp.exp2(s - m)
        l = jnp.sum(p, axis=0, keepdims=True)
        ot = _dot(vt, p.astype(BF16)) / l
        o_ref[0, :, h * DH_B:(h + 1) * DH_B] = ot.T.astype(BF16)


def _dsa_attention(qt, qit, wit, k, vt, ki, *, l_valid, q_off, n_sel, groups):
    b, nq, nf, tq = qt.shape
    lpad = k.shape[1]
    nqg = nq // groups
    tg = nqg * tq
    outs = []
    for gi in range(groups):
        q_end = q_off + (gi + 1) * tg - 1
        k_end = min(l_valid, ((q_end >> CHUNK_SHIFT) + 1) << CHUNK_SHIFT)
        lk = min(lpad, _round_up(k_end, LANES))
        qmap = lambda bi, i, gi=gi: (bi, gi * nqg + i, 0, 0)
        kmap = lambda bi, i: (bi, 0, 0)
        outs.append(pl.pallas_call(
            functools.partial(_dsa_kernel, tq=tq, lk=lk, l_valid=l_valid, q_off=q_off + gi * tg, n_sel=n_sel),
            grid=(b, nqg),
            in_specs=[
                _tiled_spec(nf, tq, qmap),
                _tiled_spec(qit.shape[2], tq, qmap),
                _tiled_spec(wit.shape[2], tq, qmap),
                pl.BlockSpec((1, lk, DH_B), kmap),
                pl.BlockSpec((1, DH_B, lk), kmap),
                pl.BlockSpec((1, lk, DH_IDX), kmap),
            ],
            out_specs=pl.BlockSpec((1, tq, nf), lambda bi, i: (bi, i, 0)),
            out_shape=jax.ShapeDtypeStruct((b, tg, nf), BF16),
            compiler_params=_cparams(("parallel", "parallel")),
            name="dsa_attention",
        )(qt, qit, wit, k, vt, ki))
    return outs[0] if groups == 1 else jnp.concatenate(outs, axis=1)


def _proj_c_kernel(x_ref, g_ref, wcq_ref, wckv_ref, wkpe_ref, gq_ref, gkv_ref, wuqt_ref,
                   ck_ref, sk_ref, ct_ref, st_ref, qt_ref, ckv_ref, kpe_ref):
    h = _rms(x_ref[...], g_ref[...]).astype(BF16)
    cq = _rms(_dot(h, wcq_ref[...]), gq_ref[...]).astype(BF16)
    ckv_ref[...] = _rms(_dot(h, wckv_ref[...]), gkv_ref[...])
    kpe_ref[...] = _rope_slab(_dot(h, wkpe_ref[...]), ck_ref[...], sk_ref[...], QK_ROPE)
    qt = _dot_nt(wuqt_ref[...], cq)
    cos, sin = ct_ref[...], st_ref[...]
    scale = (QK_NOPE + QK_ROPE) ** -0.5 * LOG2E
    half = QK_ROPE // 2
    for hh in range(H_C):
        base = hh * LANES
        r0 = base + QK_NOPE
        o1, o2 = _rope_rows(qt[r0:r0 + half, :], qt[r0 + half:r0 + QK_ROPE, :], cos, sin)
        qt_ref[0, base:r0, :] = (qt[base:r0, :] * scale).astype(BF16)
        qt_ref[0, r0:r0 + QK_ROPE, :] = (jnp.concatenate([o1, o2], axis=0) * scale).astype(BF16)
        qt_ref[0, r0 + QK_ROPE:base + LANES, :] = jnp.zeros((LANES - QK_NOPE - QK_ROPE, qt.shape[1]), BF16)


def _proj_c(x, g, w, gq, gkv, wuqt, row_tabs, time_tabs, b, t, tm=256):
    m, d = x.shape
    tm = min(tm, t)
    nt = t // tm
    in_specs = [pl.BlockSpec((tm, d), _row), _resident((1, d))]
    in_specs += [_resident(a.shape) for a in w]
    in_specs += [_resident((1, Q_LORA)), _resident((1, KV_LORA)), _resident(wuqt.shape)]
    in_specs += [pl.BlockSpec((tm, LANES), _row_tab(nt)) for _ in row_tabs]
    in_specs += [pl.BlockSpec((tb.shape[0], tm), _time_tab(nt)) for tb in time_tabs]
    nq = H_C * LANES
    return pl.pallas_call(
        _proj_c_kernel,
        grid=(m // tm,),
        in_specs=in_specs,
        out_specs=[_tiled_spec(nq, tm, _tiled_out(nt)), pl.BlockSpec((tm, KV_LORA), _row),
                   pl.BlockSpec((tm, LANES), _row)],
        out_shape=[jax.ShapeDtypeStruct((b, nt, nq, tm), BF16), jax.ShapeDtypeStruct((m, KV_LORA), F32),
                   jax.ShapeDtypeStruct((m, LANES), F32)],
        compiler_params=_cparams(("parallel",)),
        name="proj_c",
    )(x, g.reshape(1, d), *w, gq.reshape(1, Q_LORA), gkv.reshape(1, KV_LORA), wuqt, *row_tabs, *time_tabs)


def _kv_expand_kernel(ckv_ref, kpe_ref, wuk_ref, wuvt_ref, place_ref, k_ref, vt_ref):
    ckv = ckv_ref[0].astype(BF16)
    kpe = _dot(kpe_ref[0].astype(BF16), place_ref[...])
    kn = _dot(ckv, wuk_ref[...])
    for hh in range(H_C):
        sl = slice(hh * LANES, (hh + 1) * LANES)
        k_ref[0, :, sl] = (kn[:, sl] + kpe).astype(BF16)
    _store_aug_values(vt_ref, _dot_nt(wuvt_ref[...], ckv).astype(BF16))


def _kv_expand(ckv, kpe, wuk, wuvt, place, tm):
    b, l, _ = ckv.shape
    rmap = lambda bi, i: (bi, i, 0)
    cmap = lambda bi, i: (0, 0)
    return pl.pallas_call(
        _kv_expand_kernel,
        grid=(b, l // tm),
        in_specs=[pl.BlockSpec((1, tm, KV_LORA), rmap), pl.BlockSpec((1, tm, LANES), rmap),
                  pl.BlockSpec(wuk.shape, cmap), pl.BlockSpec(wuvt.shape, cmap),
                  pl.BlockSpec(place.shape, cmap)],
        out_specs=[pl.BlockSpec((1, tm, H_C * LANES), rmap),
                   _tiled_spec(H_C * VROWS, tm, lambda bi, i: (bi, i, 0, 0))],
        out_shape=[jax.ShapeDtypeStruct((b, l, H_C * LANES), BF16),
                   jax.ShapeDtypeStruct((b, l // tm, H_C * VROWS, tm), BF16)],
        compiler_params=_cparams(("parallel", "parallel")),
        name="kv_expand",
    )(ckv, kpe, wuk, wuvt, place)


def _rope_angles(pos, dim):
    half = dim // 2
    inv = ROPE_THETA ** (-jnp.arange(half, dtype=F32) / half)
    ang = pos.astype(F32)[:, None] * inv[None, :]
    return jnp.cos(ang), jnp.sin(ang)


def _rope_row_tables(pos, dim):
    cos, sin = _rope_angles(pos, dim)
    reps = LANES // dim
    return (jnp.tile(jnp.concatenate([cos, cos], axis=-1), (1, reps)),
            jnp.tile(jnp.concatenate([-sin, sin], axis=-1), (1, reps)))


def _rope_time_tables(pos, dim):
    cos, sin = _rope_angles(pos, dim)
    return cos.T, sin.T


def _prep_weights(p):
    w = {}
    w["ffn1_in"] = p["ffn1_w_in"].astype(BF16)
    w["ffn1_out"] = p["ffn1_w_out"].astype(BF16)
    w["ffn2_in"] = p["ffn2_w_in"].astype(BF16)
    w["ffn2_out"] = p["ffn2_w_out"].astype(BF16)
    a = p["a_w_in"].astype(BF16)
    n = H_A * DH_A
    w["a"] = (a[:, :n].T, a[:, n:2 * n], a[:, 2 * n:3 * n], a[:, 2 * n:3 * n].T, a[:, 3 * n:])
    w["a_out"] = p["a_w_out"].astype(BF16)
    bw = p["b_w_in"].astype(BF16)
    offs = np.cumsum([0, H_B * DH_B, DH_B, DH_B, H_IDX * DH_IDX, DH_IDX, H_IDX]).tolist()
    bq, bk, bv, bqi, bki, bwi = [bw[:, offs[i]:offs[i + 1]] for i in range(6)]
    w["b"] = (bq.T, bk, bv, bv.T, bqi.T, jnp.concatenate([bki, bki], axis=1), bwi.T)
    w["b_out"] = p["b_w_out"].astype(BF16)
    c = p["c_w_down"].astype(BF16)
    d = c.shape[0]
    wkpe = jnp.zeros((d, LANES), BF16).at[:, :QK_ROPE].set(c[:, Q_LORA + KV_LORA:])
    w["c"] = (c[:, :Q_LORA], c[:, Q_LORA:Q_LORA + KV_LORA], wkpe)
    uq = p["c_w_uq"].astype(BF16).reshape(Q_LORA, H_C, QK_NOPE + QK_ROPE)
    uq = jnp.pad(uq, ((0, 0), (0, 0), (0, LANES - QK_NOPE - QK_ROPE)))
    w["c_uqt"] = uq.reshape(Q_LORA, H_C * LANES).T
    ukv = p["c_w_ukv"].astype(BF16).reshape(KV_LORA, H_C, QK_NOPE + V_C)
    uk = jnp.pad(ukv[:, :, :QK_NOPE], ((0, 0), (0, 0), (0, LANES - QK_NOPE)))
    w["c_uk"] = uk.reshape(KV_LORA, H_C * LANES)
    w["c_uvt"] = ukv[:, :, QK_NOPE:].reshape(KV_LORA, H_C * V_C).T
    w["c_place"] = jnp.zeros((LANES, LANES), BF16).at[jnp.arange(QK_ROPE), QK_NOPE + jnp.arange(QK_ROPE)].set(1.0)
    w["c_out"] = p["c_w_out"].astype(BF16)
    dw = p["d_w_in"].astype(BF16)
    n = H_D * DH_D
    w["d"] = (dw[:, :n].T, dw[:, n:2 * n], dw[:, 2 * n:], dw[:, 2 * n:].T)
    w["d_out"] = p["d_w_out"].astype(BF16)
    return w


def _round_up(x, m):
    return (x + m - 1) // m * m


def _cat_time(past, new, lpad, axis):
    a = new if past is None else jnp.concatenate([past.astype(new.dtype), new], axis=axis)
    if a.shape[axis] < lpad:
        pad = [(0, 0)] * a.ndim
        pad[axis] = (0, lpad - a.shape[axis])
        a = jnp.pad(a, pad)
    return a


def _trunk(x3, past, p, w):
    b, t, d = x3.shape
    m = b * t
    x = x3.reshape(m, d)
    past_len = 0 if past is None else past[0][0].shape[1]
    l_valid = past_len + t
    pos = past_len + jnp.arange(t, dtype=jnp.int32)
    if past is None:
        tile = min(256, t)
        tile_s = min(512, t)
        lpad = _round_up(l_valid, tile_s)
        ktile, ktile_s, tpad = tile, tile_s, t
        groups = 8 if t % (8 * tile) == 0 else 1
    else:
        tile = tile_s = tpad = _round_up(t, LANES)
        lpad = ktile_s = _round_up(l_valid, LANES)
        ktile = LANES
        groups = 1
    norm_g = p["norm_g"]
    rows = {}

    def r3(a):
        return a.reshape(b, t, a.shape[-1])

    def q_in(a4):
        return a4 if tpad == t else jnp.pad(a4, ((0, 0), (0, 0), (0, 0), (0, tpad - t)))

    def v_in(vt4, past_t, kt):
        if past is None:
            return vt4
        a = _cat_time(past_t, vt4.reshape(b, vt4.shape[2], t), lpad, 2)
        return jnp.transpose(a.reshape(b, a.shape[1], lpad // kt, kt), (0, 2, 1, 3))

    def unpad(o):
        return (o if tpad == t else o[:, :t]).reshape(m, -1)

    def pst(idx, j, width):
        return None if past is None else past[idx][j].reshape(b, past_len, width)

    def pst_t(idx, j, width):
        return None if past is None else jnp.transpose(past[idx][j].reshape(b, past_len, width), (0, 2, 1))

    def aug(vt_past):
        if vt_past is None:
            return None
        dv = LANES // 2
        v4 = vt_past.reshape(b, -1, dv, past_len)
        extra = jnp.zeros((b, v4.shape[1], VROWS - dv, past_len), v4.dtype).at[:, :, 0, :].set(1.0)
        return jnp.concatenate([v4, extra], axis=2).reshape(b, -1, past_len)

    for i in range(4):
        x = _ffn(x, norm_g[i, 0], w["ffn1_in"][i], w["ffn1_out"][i])
        g = norm_g[i, 1]
        if i == 0:
            wqt, wk, wv, wvt, wf = w["a"]
            qt, k, kb, v, vt, lf = _proj_qkv(x, g, wqt, wk, wv, wvt, DH_A ** -0.5 * LOG2E, b, t, wf, p["a_b_f"],
                                             tm=tile_s)
            rows["a"] = (k.reshape(b, t, H_A, DH_A), v.reshape(b, t, H_A, DH_A), lf.reshape(b, t, H_A))
            lf_all = _cat_time(None if past is None else past[0][2].astype(F32), r3(lf), lpad, 1)
            c = _cumsum_time(jnp.pad(lf_all, ((0, 0), (0, 0), (0, LANES - H_A))))
            o = _attention(q_in(qt), _cat_time(pst(0, 0, H_A * DH_A), r3(kb), lpad, 1),
                           v_in(vt, aug(pst_t(0, 1, H_A * DH_A)), ktile_s), c,
                           nh=H_A, k_per_head=False, chunked=False, q_off=past_len, l_valid=l_valid)
            mix = (unpad(o), w["a_out"])
        elif i == 1:
            row_tabs = list(_rope_row_tables(pos, DH_B)) + list(_rope_row_tables(pos, DH_IDX))
            time_tabs = list(_rope_time_tables(pos, DH_B)) + list(_rope_time_tables(pos, DH_IDX))
            qt, k, kb, v, vt, qit, ki, kib, wit = _proj_b(x, g, w["b"], row_tabs, time_tabs, b, t, tm=tile)
            rows["b"] = (r3(k), r3(v), r3(ki))
            n_sel = min(IDX_TOPK, l_valid // 4)
            o = _dsa_attention(q_in(qt), q_in(qit), q_in(wit), _cat_time(pst(1, 0, DH_B), r3(kb), lpad, 1),
                               _cat_time(pst_t(1, 1, DH_B), vt, lpad, 2),
                               _cat_time(pst(1, 2, DH_IDX), r3(kib), lpad, 1),
                               l_valid=l_valid, q_off=past_len, n_sel=n_sel, groups=groups)
            mix = (unpad(o), w["b_out"])
        elif i == 2:
            row_tabs = list(_rope_row_tables(pos, QK_ROPE))
            time_tabs = list(_rope_time_tables(pos, QK_ROPE))
            qt, ckv, kpe = _proj_c(x, g, w["c"], p["c_g_q"], p["c_g_kv"], w["c_uqt"], row_tabs, time_tabs, b, t,
                                   tm=tile_s)
            rows["c"] = (r3(ckv), r3(kpe)[:, :, :QK_ROPE])
            kpe_past = None if past is None else jnp.pad(past[2][1].astype(F32),
                                                         ((0, 0), (0, 0), (0, LANES - QK_ROPE)))
            ckv_all = _cat_time(None if past is None else past[2][0].astype(F32), r3(ckv), lpad, 1)
            kpe_all = _cat_time(kpe_past, r3(kpe), lpad, 1)
            kc, vct = _kv_expand(ckv_all, kpe_all, w["c_uk"], w["c_uvt"], w["c_place"], tm=ktile_s)
            o = _attention(q_in(qt), kc, vct, None,
                           nh=H_C, k_per_head=True, chunked=True, q_off=past_len, l_valid=l_valid)
            mix = (unpad(o), w["c_out"])
        else:
            wqt, wk, wv, wvt = w["d"]
            qt, k, kb, v, vt = _proj_qkv(x, g, wqt, wk, wv, wvt, DH_D ** -0.5 * LOG2E, b, t, tm=tile)
            rows["d"] = (k.reshape(b, t, H_D, DH_D), v.reshape(b, t, H_D, DH_D))
            o = _sb_attention(q_in(qt), _cat_time(pst(3, 0, H_D * DH_D), r3(kb), lpad, 1),
                              v_in(vt, pst_t(3, 1, H_D * DH_D), ktile), nh=H_D, q_off=past_len)
            mix = (unpad(o), w["d_out"])
        x = _ffn(x, norm_g[i, 2], w["ffn2_in"][i], w["ffn2_out"][i], mix, p["final_g"] if i == 3 else None)
    return x.reshape(b, t, d), rows


def kernel(x_prompt, x_sample, cache_a_k, cache_a_v, cache_a_logf, cache_b_k, cache_b_v, cache_b_kidx,
           cache_c_ckv, cache_c_kpe, cache_d_k, cache_d_v, norm_g, final_g,
           ffn1_w_in, ffn1_w_out, ffn2_w_in, ffn2_w_out, a_w_in, a_b_f, a_w_out, b_w_in, b_w_out,
           c_w_down, c_g_q, c_g_kv, c_w_uq, c_w_ukv, c_w_out, d_w_in, d_w_out):
    p = dict(norm_g=norm_g, final_g=final_g, ffn1_w_in=ffn1_w_in, ffn1_w_out=ffn1_w_out,
             ffn2_w_in=ffn2_w_in, ffn2_w_out=ffn2_w_out, a_w_in=a_w_in, a_b_f=a_b_f, a_w_out=a_w_out,
             b_w_in=b_w_in, b_w_out=b_w_out, c_w_down=c_w_down, c_g_q=c_g_q, c_g_kv=c_g_kv,
             c_w_uq=c_w_uq, c_w_ukv=c_w_ukv, c_w_out=c_w_out, d_w_in=d_w_in, d_w_out=d_w_out)
    w = _prep_weights(p)
    y_prompt, pr = _trunk(x_prompt, None, p, w)
    past = ((cache_a_k, cache_a_v, cache_a_logf), (cache_b_k, cache_b_v, cache_b_kidx),
            (cache_c_ckv, cache_c_kpe), (cache_d_k, cache_d_v))
    y_sample, sr = _trunk(x_sample, past, p, w)
    return (y_prompt, y_sample, *pr["a"], *pr["b"], *pr["c"], *pr["d"],
            *sr["a"], *sr["b"], *sr["c"], *sr["d"])
```

```python
import functools

import jax
import jax.numpy as jnp
import numpy as np
from jax import lax
from jax.experimental import pallas as pl
from jax.experimental.pallas import tpu as pltpu

F32 = jnp.float32
BF16 = jnp.bfloat16

D_MODEL = 1024
CHUNK_SHIFT = 6
ROPE_THETA = 10000.0
RMS_EPS = 1e-6
D_FF = 2816
H_A, DH_A = 16, 64
H_B, DH_B = 8, 128
H_IDX, DH_IDX = 8, 64
IDX_TOPK = 256
H_C, Q_LORA, KV_LORA, QK_NOPE, QK_ROPE, V_C = 16, 256, 128, 64, 32, 64
H_D, DH_D = 16, 64

LANES = 128
NEG = -1e30
LOG2E = 1.4426950408889634
EXP2_ZERO = -150.0
VROWS = 80
VMEM_LIMIT = 56 * 1024 * 1024


def _cparams(sem):
    return pltpu.CompilerParams(dimension_semantics=sem, vmem_limit_bytes=VMEM_LIMIT)


def _dot(a, b):
    return jnp.dot(a, b, preferred_element_type=F32)


def _dot_nt(a, b):
    return lax.dot_general(a, b, (((1,), (1,)), ((), ())), preferred_element_type=F32)


def _rms(x, g):
    return x * lax.rsqrt(jnp.mean(x * x, axis=-1, keepdims=True) + RMS_EPS) * g


def _softplus_neg_abs(z):
    return jnp.log(1.0 + jnp.exp(-jnp.abs(z)))


def _log_sigmoid(z):
    return jnp.minimum(z, 0.0) - _softplus_neg_abs(z)


def _rope_slab(x, cos, sin, group):
    half = group // 2
    if group == LANES:
        partner = pltpu.roll(x, half, 1)
    else:
        lane = lax.broadcasted_iota(jnp.int32, (1, LANES), 1)
        first = (lane & (group - 1)) < half
        partner = jnp.where(first, pltpu.roll(x, LANES - half, 1), pltpu.roll(x, half, 1))
    return x * cos + partner * sin


def _store_aug_values(vt_ref, vt):
    dv = LANES // 2
    n = vt.shape[1]
    extra = jnp.where(lax.broadcasted_iota(jnp.int32, (VROWS - dv, n), 0) == 0, 1.0, 0.0).astype(BF16)
    for hh in range(vt.shape[0] // dv):
        vt_ref[0, hh * VROWS:hh * VROWS + dv, :] = vt[hh * dv:(hh + 1) * dv, :]
        vt_ref[0, hh * VROWS + dv:(hh + 1) * VROWS, :] = extra


def _rope_rows(x1, x2, cos, sin):
    return x1 * cos - x2 * sin, x1 * sin + x2 * cos


FFN_CHUNK = 256


def _ffn_kernel(x_ref, *rest, mix, final):
    rest = list(rest)
    if mix:
        o_ref, wmix_ref = rest.pop(0), rest.pop(0)
    g_ref, win_ref, wo_ref = rest.pop(0), rest.pop(0), rest.pop(0)
    if final:
        fg_ref = rest.pop(0)
    (y_ref,) = rest
    dff = wo_ref.shape[0]
    x = x_ref[...]
    if mix:
        x = x + _dot(o_ref[...], wmix_ref[...])
    h = _rms(x, g_ref[...]).astype(BF16)

    def inproj(c):
        lo = c * FFN_CHUNK
        return _dot(h, win_ref[:, lo:lo + FFN_CHUNK]), _dot(h, win_ref[:, dff + lo:dff + lo + FFN_CHUNK])

    acc = None
    nxt = inproj(0)
    for c in range(dff // FFN_CHUNK):
        up, gate = nxt
        if (c + 1) * FFN_CHUNK < dff:
            nxt = inproj(c + 1)
        act = (gate / (1.0 + jnp.exp(-gate)) * up).astype(BF16)
        part = _dot(act, wo_ref[c * FFN_CHUNK:(c + 1) * FFN_CHUNK, :])
        acc = part if acc is None else acc + part
    y = x + 0.5 * acc
    if final:
        y = _rms(y, fg_ref[...])
    y_ref[...] = y


def _resident(shape):
    return pl.BlockSpec(shape, lambda i: (0,) * len(shape), pipeline_mode=pl.Buffered(1))


def _ffn(x, g, w_in, w_out, mix=None, final_g=None, tm=512):
    m, d = x.shape
    dff = w_out.shape[0]
    tm = min(tm, m)
    row = pl.BlockSpec((tm, d), lambda i: (i, 0))
    in_specs, args = [row], [x]
    if mix is not None:
        o, wmix = mix
        in_specs += [pl.BlockSpec((tm, o.shape[1]), lambda i: (i, 0)), _resident(wmix.shape)]
        args += [o, wmix]
    in_specs += [_resident((1, d)), _resident((d, 2 * dff)), _resident((dff, d))]
    args += [g.reshape(1, d), w_in, w_out]
    if final_g is not None:
        in_specs.append(_resident((1, d)))
        args.append(final_g.reshape(1, d))
    return pl.pallas_call(
        functools.partial(_ffn_kernel, mix=mix is not None, final=final_g is not None),
        grid=(m // tm,),
        in_specs=in_specs,
        out_specs=row,
        out_shape=jax.ShapeDtypeStruct((m, d), F32),
        compiler_params=_cparams(("parallel",)),
        name="ffn",
    )(*args)


def _row(i):
    return (i, 0)


def _feat_time(nt):
    return lambda i: (i // nt, 0, i % nt)


def _tiled_spec(feat, tile, index_map):
    return pl.BlockSpec((None, 1, feat, tile), index_map)


def _tiled_out(nt):
    return lambda i: (i // nt, i % nt, 0, 0)


def _time_tab(nt):
    return lambda i: (0, i % nt)


def _row_tab(nt):
    return lambda i: (i % nt, 0)


def _proj_qkv_kernel(x_ref, g_ref, wqt_ref, wk_ref, wv_ref, wvt_ref, *rest, scale, has_f):
    if has_f:
        wf_ref, bf_ref, qt_ref, k_ref, kb_ref, v_ref, vt_ref, lf_ref = rest
    else:
        qt_ref, k_ref, kb_ref, v_ref, vt_ref = rest
    h = _rms(x_ref[...], g_ref[...]).astype(BF16)
    qt = (_dot_nt(wqt_ref[...], h) * scale).astype(BF16)
    half = LANES // 2
    zeros = jnp.zeros((half, qt.shape[1]), BF16)
    for hh in range(qt.shape[0] // half):
        base = hh * LANES
        own = slice(base + (hh % 2) * half, base + (hh % 2) * half + half)
        other = slice(base + (1 - hh % 2) * half, base + (1 - hh % 2) * half + half)
        qt_ref[0, own, :] = qt[hh * half:(hh + 1) * half, :]
        qt_ref[0, other, :] = zeros
    k = _dot(h, wk_ref[...])
    k_ref[...] = k.reshape(k_ref.shape)
    kb_ref[...] = k.astype(BF16)
    v = _dot(h, wv_ref[...])
    v_ref[...] = v.reshape(v_ref.shape)
    if v.shape[0] % LANES == 0:
        vt = v.T.astype(BF16)
    else:
        vt = _dot_nt(wvt_ref[...], h).astype(BF16)
    if has_f:
        _store_aug_values(vt_ref, vt)
        lf_ref[...] = _log_sigmoid(_dot(h, wf_ref[...]) + bf_ref[...])
    else:
        vt_ref[0] = vt


def _proj_qkv(x, g, wqt, wk, wv, wvt, scale, b, t, wf=None, bf=None, tm=256):
    m, d = x.shape
    n = wk.shape[1]
    tm = min(tm, t)
    nt = t // tm
    has_f = wf is not None
    nvt = n // (LANES // 2) * VROWS if has_f else n
    in_specs = [pl.BlockSpec((tm, d), _row), _resident((1, d))] + [_resident((d, n))] * 4
    args = [x, g.reshape(1, d), wqt, wk, wv, wvt]
    dh = LANES // 2
    heads = pl.BlockSpec((tm, n // dh, dh), lambda i: (i, 0, 0))
    out_specs = [_tiled_spec(2 * n, tm, _tiled_out(nt)), heads,
                 pl.BlockSpec((tm, n), _row), heads, _tiled_spec(nvt, tm, _tiled_out(nt))]
    out_shape = [jax.ShapeDtypeStruct((b, nt, 2 * n, tm), BF16), jax.ShapeDtypeStruct((m, n // dh, dh), F32),
                 jax.ShapeDtypeStruct((m, n), BF16), jax.ShapeDtypeStruct((m, n // dh, dh), F32),
                 jax.ShapeDtypeStruct((b, nt, nvt, tm), BF16)]
    if has_f:
        nh = wf.shape[1]
        in_specs += [_resident((d, nh)), _resident((1, nh))]
        args += [wf, bf.reshape(1, nh)]
        out_specs.append(pl.BlockSpec((tm, nh), _row))
        out_shape.append(jax.ShapeDtypeStruct((m, nh), F32))
    return pl.pallas_call(
        functools.partial(_proj_qkv_kernel, scale=scale, has_f=has_f),
        grid=(m // tm,),
        in_specs=in_specs,
        out_specs=out_specs,
        out_shape=out_shape,
        compiler_params=_cparams(("parallel",)),
        name="proj_qkv",
    )(*args)


def _split3(x):
    hi = x.astype(BF16)
    r1 = x - hi.astype(F32)
    mid = r1.astype(BF16)
    lo = (r1 - mid.astype(F32)).astype(BF16)
    return hi, mid, lo


def _cumsum_kernel(x_ref, c_ref, *, nblk):
    r = lax.broadcasted_iota(jnp.int32, (LANES, LANES), 0)
    c = lax.broadcasted_iota(jnp.int32, (LANES, LANES), 1)
    tri = jnp.where(c <= r, 1.0, 0.0).astype(BF16)
    carry = jnp.zeros((1, x_ref.shape[2]), F32)
    for b in range(nblk):
        hi, mid, lo = _split3(x_ref[0, b * LANES:(b + 1) * LANES, :])
        blk = (_dot(tri, hi) + _dot(tri, mid)) + _dot(tri, lo) + carry
        c_ref[0, b * LANES:(b + 1) * LANES, :] = blk
        carry = blk[LANES - 1:LANES, :]


def _cumsum_time(x):
    b, l, nh = x.shape
    spec = pl.BlockSpec((1, l, nh), lambda i: (i, 0, 0))
    return pl.pallas_call(
        functools.partial(_cumsum_kernel, nblk=l // LANES),
        grid=(b,),
        in_specs=[spec],
        out_specs=spec,
        out_shape=jax.ShapeDtypeStruct((b, l, nh), F32),
        compiler_params=_cparams(("parallel",)),
        name="cumsum_time",
    )(x)


def _last_block(i, tq, tk, q_off, chunked, nk):
    q_last = q_off + (i + 1) * tq - 1
    if chunked:
        q_last = ((q_last >> CHUNK_SHIFT) << CHUNK_SHIFT) + (1 << CHUNK_SHIFT) - 1
    return jnp.minimum(q_last // tk, nk - 1)


def _block_pairs(nq, tq, tk, q_off, chunked, nk, reverse):
    ii, jj = [], []
    for i in range(nq):
        q_last = q_off + (i + 1) * tq - 1
        if chunked:
            q_last = ((q_last >> CHUNK_SHIFT) << CHUNK_SHIFT) + (1 << CHUNK_SHIFT) - 1
        blocks = list(range(min(q_last // tk, nk - 1) + 1))
        for j in (reversed(blocks) if reverse else blocks):
            ii.append(i)
            jj.append(j)
    return jnp.asarray(ii, jnp.int32), jnp.asarray(jj, jnp.int32)


def _attn_kernel(ii_ref, jj_ref, qt_ref, k_ref, vt_ref, *rest, nh, k_per_head, bias, chunked, tq, tk, q_off, l_valid,
                 nk):
    if bias:
        c_ref, o_ref, m_ref, acc_ref = rest
    else:
        o_ref, m_ref, acc_ref = rest
    i = ii_ref[pl.program_id(1)]
    j = jj_ref[pl.program_id(1)]
    dv = LANES // 2

    @pl.when(j == 0)
    def _():
        m_ref[...] = jnp.full_like(m_ref, NEG)
        acc_ref[...] = jnp.zeros_like(acc_ref)

    q_first = q_off + i * tq
    k_last = j * tk + tk - 1
    if chunked:
        full = ((k_last >> CHUNK_SHIFT) <= (q_first >> CHUNK_SHIFT)) & (k_last < l_valid)
    else:
        full = k_last <= q_first

    def step(masked):
        if masked:
            kpos = j * tk + lax.broadcasted_iota(jnp.int32, (tk, tq), 0)
            qpos = q_first + lax.broadcasted_iota(jnp.int32, (tk, tq), 1)
            if chunked:
                valid = ((kpos >> CHUNK_SHIFT) <= (qpos >> CHUNK_SHIFT)) & (kpos < l_valid)
            else:
                valid = kpos <= qpos
            maskbias = jnp.where(valid, 0.0, NEG)
        def qk(h):
            ks = h if k_per_head else h // 2
            return _dot(k_ref[0, :, ks * LANES:(ks + 1) * LANES], qt_ref[0, h * LANES:(h + 1) * LANES, :])

        s_next = qk(0)
        for h in range(nh):
            s = s_next
            if h + 1 < nh:
                s_next = qk(h + 1)
            if bias:
                s = s - c_ref[0, :, h:h + 1] * LOG2E
            if masked:
                s = s + maskbias
            m_prev = m_ref[h:h + 1, :]
            m_new = jnp.maximum(m_prev, jnp.max(s, axis=0, keepdims=True))
            alpha = jnp.exp2(m_prev - m_new)
            p = jnp.exp2(s - m_new)
            m_ref[h:h + 1, :] = m_new
            rows = slice(h * VROWS, (h + 1) * VROWS)
            acc_ref[rows, :] = alpha * acc_ref[rows, :] + _dot(vt_ref[0, rows, :], p.astype(BF16))

    @pl.when(full)
    def _():
        step(False)

    @pl.when(jnp.logical_not(full))
    def _():
        step(True)

    @pl.when(j == _last_block(i, tq, tk, q_off, chunked, nk))
    def _():
        for p in range(nh // 2):
            parts = [acc_ref[h * VROWS:h * VROWS + dv, :] / acc_ref[h * VROWS + dv:h * VROWS + dv + 1, :]
                     for h in (2 * p, 2 * p + 1)]
            o_ref[0, :, p * LANES:(p + 1) * LANES] = jnp.concatenate(parts, axis=0).T.astype(BF16)


def _attention(qt, k, vt, c, *, nh, k_per_head, chunked, q_off, l_valid):
    b, nq, _, tq = qt.shape
    nk, tk = vt.shape[1], vt.shape[3]
    assert k.shape[1] == nk * tk
    bias = c is not None
    ii, jj = _block_pairs(nq, tq, tk, q_off, chunked, nk, reverse=False)
    in_specs = [
        _tiled_spec(qt.shape[2], tq, lambda bi, p, ii, jj: (bi, ii[p], 0, 0)),
        pl.BlockSpec((1, tk, k.shape[2]), lambda bi, p, ii, jj: (bi, jj[p], 0)),
        _tiled_spec(vt.shape[2], tk, lambda bi, p, ii, jj: (bi, jj[p], 0, 0)),
    ]
    args = [qt, k, vt]
    if bias:
        in_specs.append(pl.BlockSpec((1, tk, c.shape[2]), lambda bi, p, ii, jj: (bi, jj[p], 0)))
        args.append(c)
    dv = nh * (LANES // 2)
    return pl.pallas_call(
        functools.partial(_attn_kernel, nh=nh, k_per_head=k_per_head, bias=bias, chunked=chunked,
                          tq=tq, tk=tk, q_off=q_off, l_valid=l_valid, nk=nk),
        grid_spec=pltpu.PrefetchScalarGridSpec(
            num_scalar_prefetch=2,
            grid=(b, int(ii.shape[0])),
            in_specs=in_specs,
            out_specs=pl.BlockSpec((1, tq, dv), lambda bi, p, ii, jj: (bi, ii[p], 0)),
            scratch_shapes=[pltpu.VMEM((nh, tq), F32), pltpu.VMEM((vt.shape[2], tq), F32)]),
        out_shape=jax.ShapeDtypeStruct((b, nq * tq, dv), BF16),
        compiler_params=_cparams(("parallel", "arbitrary")),
        name="attention",
    )(ii, jj, *args)


def _sb_kernel(ii_ref, jj_ref, qt_ref, k_ref, vt_ref, o_ref, tail_ref, acc_ref, *, nh, tq, tk, q_off, nk):
    i = ii_ref[pl.program_id(1)]
    jj = jj_ref[pl.program_id(1)]
    dv = LANES // 2

    @pl.when(jj == _last_block(i, tq, tk, q_off, False, nk))
    def _():
        tail_ref[...] = jnp.zeros_like(tail_ref)
        acc_ref[...] = jnp.zeros_like(acc_ref)

    q_first = q_off + i * tq
    full = jj * tk + tk - 1 < q_first
    active = jnp.max(tail_ref[...]) > EXP2_ZERO

    def step(masked):
        r = lax.broadcasted_iota(jnp.int32, (tk + 16, tk), 0)
        c = lax.broadcasted_iota(jnp.int32, (tk + 16, tk), 1)
        after = jnp.where(((r < tk) & (c > r)) | (r == tk), 1.0, 0.0).astype(BF16)
        if masked:
            kpos = jj * tk + lax.broadcasted_iota(jnp.int32, (tk, tq), 0)
            qpos = q_first + lax.broadcasted_iota(jnp.int32, (tk, tq), 1)
            valid = kpos < qpos

        def qk(h):
            return _dot(k_ref[0, :, (h // 2) * LANES:(h // 2 + 1) * LANES], qt_ref[0, h * LANES:(h + 1) * LANES, :])

        def logs(z):
            log_beta = jnp.minimum(z, 0.0) - jnp.log2(1.0 + jnp.exp2(-jnp.abs(z)))
            log_1mb = log_beta - z
            if masked:
                log_1mb = jnp.where(valid, log_1mb, 0.0)
            return log_beta, _dot(after, log_1mb.astype(BF16))

        def finish(h, log_beta, sums):
            t_prev = tail_ref[h:h + 1, :]
            a = jnp.exp2(log_beta + sums[:tk, :] + t_prev)
            if masked:
                a = jnp.where(valid, a, 0.0)
            tail_ref[h:h + 1, :] = t_prev + sums[tk:tk + 1, :]
            rows = slice(h * dv, (h + 1) * dv)
            acc_ref[rows, :] += _dot(vt_ref[0, rows, :], a.astype(BF16))

        z_next = qk(0)
        pending = None
        for h in range(nh):
            z = z_next
            if h + 1 < nh:
                z_next = qk(h + 1)
            cur = logs(z)
            if pending is not None:
                finish(h - 1, *pending)
            pending = cur
        finish(nh - 1, *pending)

    @pl.when(active & full)
    def _():
        step(False)

    @pl.when(active & jnp.logical_not(full))
    def _():
        step(True)

    @pl.when(jj == 0)
    def _():
        for p in range(nh // 2):
            o_ref[0, :, p * LANES:(p + 1) * LANES] = acc_ref[p * LANES:(p + 1) * LANES, :].T.astype(BF16)


def _sb_attention(qt, k, vt, *, nh, q_off):
    b, nq, _, tq = qt.shape
    nk, dv, tk = vt.shape[1:]
    assert k.shape[1] == nk * tk

    ii, jj = _block_pairs(nq, tq, tk, q_off, False, nk, reverse=True)
    return pl.pallas_call(
        functools.partial(_sb_kernel, nh=nh, tq=tq, tk=tk, q_off=q_off, nk=nk),
        grid_spec=pltpu.PrefetchScalarGridSpec(
            num_scalar_prefetch=2,
            grid=(b, int(ii.shape[0])),
            in_specs=[
                _tiled_spec(qt.shape[2], tq, lambda bi, p, ii, jj: (bi, ii[p], 0, 0)),
                pl.BlockSpec((1, tk, k.shape[2]), lambda bi, p, ii, jj: (bi, jj[p], 0)),
                _tiled_spec(dv, tk, lambda bi, p, ii, jj: (bi, jj[p], 0, 0)),
            ],
            out_specs=pl.BlockSpec((1, tq, dv), lambda bi, p, ii, jj: (bi, ii[p], 0)),
            scratch_shapes=[pltpu.VMEM((nh, tq), F32), pltpu.VMEM((dv, tq), F32)]),
        out_shape=jax.ShapeDtypeStruct((b, nq * tq, dv), BF16),
        compiler_params=_cparams(("parallel", "arbitrary")),
        name="sb_attention",
    )(ii, jj, qt, k, vt)


def _proj_b_kernel(x_ref, g_ref, wqt_ref, wk_ref, wv_ref, wvt_ref, wqit_ref, wki_ref, wwit_ref,
                   c128_ref, s128_ref, c64_ref, s64_ref, ct64_ref, st64_ref, ct32_ref, st32_ref,
                   qt_ref, k_ref, kb_ref, v_ref, vt_ref, qit_ref, ki_ref, kib_ref, wit_ref):
    h = _rms(x_ref[...], g_ref[...]).astype(BF16)
    qt = _dot_nt(wqt_ref[...], h)
    cos, sin = ct64_ref[...], st64_ref[...]
    half = DH_B // 2
    for hh in range(H_B):
        base = hh * DH_B
        o1, o2 = _rope_rows(qt[base:base + half, :], qt[base + half:base + DH_B, :], cos, sin)
        qt_ref[0, base:base + half, :] = (o1 * (DH_B ** -0.5 * LOG2E)).astype(BF16)
        qt_ref[0, base + half:base + DH_B, :] = (o2 * (DH_B ** -0.5 * LOG2E)).astype(BF16)
    k = _rope_slab(_dot(h, wk_ref[...]), c128_ref[...], s128_ref[...], DH_B)
    k_ref[...] = k
    kb_ref[...] = k.astype(BF16)
    v_ref[...] = _dot(h, wv_ref[...])
    vt_ref[0] = _dot_nt(wvt_ref[...], h).astype(BF16)
    qit = _dot_nt(wqit_ref[...], h)
    cos, sin = ct32_ref[...], st32_ref[...]
    half = DH_IDX // 2
    for hh in range(H_IDX):
        base = hh * DH_IDX
        o1, o2 = _rope_rows(qit[base:base + half, :], qit[base + half:base + DH_IDX, :], cos, sin)
        qit_ref[0, base:base + half, :] = (o1 * (DH_IDX ** -0.5)).astype(BF16)
        qit_ref[0, base + half:base + DH_IDX, :] = (o2 * (DH_IDX ** -0.5)).astype(BF16)
    ki = _rope_slab(_dot(h, wki_ref[...]), c64_ref[...], s64_ref[...], DH_IDX)[:, :DH_IDX]
    ki_ref[...] = ki
    kib_ref[...] = ki.astype(BF16)
    wit_ref[0] = _dot_nt(wwit_ref[...], h) * (H_IDX ** -0.5)


def _proj_b(x, g, w, row_tabs, time_tabs, b, t, tm=256):
    m, d = x.shape
    tm = min(tm, t)
    nt = t // tm
    in_specs = [pl.BlockSpec((tm, d), _row), _resident((1, d))]
    in_specs += [_resident(a.shape) for a in w]
    in_specs += [pl.BlockSpec((tm, LANES), _row_tab(nt)) for _ in row_tabs]
    in_specs += [pl.BlockSpec((tb.shape[0], tm), _time_tab(nt)) for tb in time_tabs]
    nq, ni = H_B * DH_B, H_IDX * DH_IDX
    out_specs = [_tiled_spec(nq, tm, _tiled_out(nt)), pl.BlockSpec((tm, DH_B), _row),
                 pl.BlockSpec((tm, DH_B), _row), pl.BlockSpec((tm, DH_B), _row),
                 pl.BlockSpec((1, DH_B, tm), _feat_time(nt)), _tiled_spec(ni, tm, _tiled_out(nt)),
                 pl.BlockSpec((tm, DH_IDX), _row), pl.BlockSpec((tm, DH_IDX), _row),
                 _tiled_spec(H_IDX, tm, _tiled_out(nt))]
    out_shape = [jax.ShapeDtypeStruct((b, nt, nq, tm), BF16), jax.ShapeDtypeStruct((m, DH_B), F32),
                 jax.ShapeDtypeStruct((m, DH_B), BF16), jax.ShapeDtypeStruct((m, DH_B), F32),
                 jax.ShapeDtypeStruct((b, DH_B, t), BF16), jax.ShapeDtypeStruct((b, nt, ni, tm), BF16),
                 jax.ShapeDtypeStruct((m, DH_IDX), F32), jax.ShapeDtypeStruct((m, DH_IDX), BF16),
                 jax.ShapeDtypeStruct((b, nt, H_IDX, tm), F32)]
    return pl.pallas_call(
        _proj_b_kernel,
        grid=(m // tm,),
        in_specs=in_specs,
        out_specs=out_specs,
        out_shape=out_shape,
        compiler_params=_cparams(("parallel",)),
        name="proj_b",
    )(x, g.reshape(1, d), *w, *row_tabs, *time_tabs)


def _dsa_kernel(qt_ref, qit_ref, wit_ref, k_ref, vt_ref, ki_ref, o_ref, *, tq, lk, l_valid, q_off, n_sel):
    i = pl.program_id(1)
    kpos = lax.broadcasted_iota(jnp.int32, (lk, tq), 0)
    qpos = q_off + i * tq + lax.broadcasted_iota(jnp.int32, (lk, tq), 1)
    valid = ((kpos >> CHUNK_SHIFT) <= (qpos >> CHUNK_SHIFT)) & (kpos < l_valid)

    ki = ki_ref[0]
    score = jnp.zeros((lk, tq), F32)
    for h in range(H_IDX):
        d = _dot(ki, qit_ref[0, h * DH_IDX:(h + 1) * DH_IDX, :])
        score = score + wit_ref[0, h:h + 1, :] * jnp.maximum(d, 0.0)
    score = jnp.where(valid, score + 0.0, -jnp.inf)

    bits = pltpu.bitcast(score, jnp.int32)
    key = jnp.where(bits < 0, bits ^ jnp.int32(0x7FFFFFFF), bits)

    int_min = jnp.int32(-2 ** 31)
    nsel_f = jnp.float32(n_sel)

    def count(pred):
        return jnp.sum(jnp.where(pred, 1.0, 0.0), axis=0, keepdims=True)

    def thr_body(it, t):
        cand = t + lax.shift_left(jnp.int32(1), jnp.int32(31) - it)
        return jnp.where(count(key >= cand) >= nsel_f, cand, t)

    thr = lax.fori_loop(0, 32, thr_body, jnp.full((1, tq), int_min, jnp.int32))

    nbits = int(lk).bit_length()

    def tie_bound():
        need = nsel_f - count(key > thr)

        def tie_body(it, bound):
            cand = bound + lax.shift_left(jnp.int32(1), jnp.int32(nbits - 1) - it)
            return jnp.where(count((key == thr) & (kpos < cand)) <= need, cand, bound)

        return lax.fori_loop(0, nbits, tie_body, jnp.zeros((1, tq), jnp.int32))

    excess = jnp.max(count(key >= thr)) > nsel_f
    bound = lax.cond(excess, tie_bound, lambda: jnp.full((1, tq), 1 << nbits, jnp.int32))

    sel = ((key > thr) | ((key == thr) & (kpos < bound))) & valid
    selbias = jnp.where(sel, 0.0, NEG)

    k = k_ref[0]
    vt = vt_ref[0]

    def qk(h):
        return _dot(k, qt_ref[0, h * DH_B:(h + 1) * DH_B, :])

    s_next = qk(0)
    for h in range(H_B):
        s = s_next + selbias
        if h + 1 < H_B:
            s_next = qk(h + 1)
        m = jnp.max(s, axis=0, keepdims=True)
        p = jnp.exp2(s - m)
        l = jnp.sum(p, axis=0, keepdims=True)
        ot = _dot(vt, p.astype(BF16)) / l
        o_ref[0, :, h * DH_B:(h + 1) * DH_B] = ot.T.astype(BF16)


def _dsa_attention(qt, qit, wit, k, vt, ki, *, l_valid, q_off, n_sel, groups):
    b, nq, nf, tq = qt.shape
    lpad = k.shape[1]
    nqg = nq // groups
    tg = nqg * tq
    outs = []
    for gi in range(groups):
        q_end = q_off + (gi + 1) * tg - 1
        k_end = min(l_valid, ((q_end >> CHUNK_SHIFT) + 1) << CHUNK_SHIFT)
        lk = min(lpad, _round_up(k_end, LANES))
        qmap = lambda bi, i, gi=gi: (bi, gi * nqg + i, 0, 0)
        kmap = lambda bi, i: (bi, 0, 0)
        outs.append(pl.pallas_call(
            functools.partial(_dsa_kernel, tq=tq, lk=lk, l_valid=l_valid, q_off=q_off + gi * tg, n_sel=n_sel),
            grid=(b, nqg),
            in_specs=[
                _tiled_spec(nf, tq, qmap),
                _tiled_spec(qit.shape[2], tq, qmap),
                _tiled_spec(wit.shape[2], tq, qmap),
                pl.BlockSpec((1, lk, DH_B), kmap),
                pl.BlockSpec((1, DH_B, lk), kmap),
                pl.BlockSpec((1, lk, DH_IDX), kmap),
            ],
            out_specs=pl.BlockSpec((1, tq, nf), lambda bi, i: (bi, i, 0)),
            out_shape=jax.ShapeDtypeStruct((b, tg, nf), BF16),
            compiler_params=_cparams(("parallel", "parallel")),
            name="dsa_attention",
        )(qt, qit, wit, k, vt, ki))
    return outs[0] if groups == 1 else jnp.concatenate(outs, axis=1)


def _proj_c_kernel(x_ref, g_ref, wcq_ref, wckv_ref, wkpe_ref, gq_ref, gkv_ref, wuqt_ref,
                   ck_ref, sk_ref, ct_ref, st_ref, qt_ref, ckv_ref, kpe_ref):
    h = _rms(x_ref[...], g_ref[...]).astype(BF16)
    cq = _rms(_dot(h, wcq_ref[...]), gq_ref[...]).astype(BF16)
    ckv_ref[...] = _rms(_dot(h, wckv_ref[...]), gkv_ref[...])
    kpe_ref[...] = _rope_slab(_dot(h, wkpe_ref[...]), ck_ref[...], sk_ref[...], QK_ROPE)
    qt = _dot_nt(wuqt_ref[...], cq)
    cos, sin = ct_ref[...], st_ref[...]
    scale = (QK_NOPE + QK_ROPE) ** -0.5 * LOG2E
    half = QK_ROPE // 2
    for hh in range(H_C):
        base = hh * LANES
        r0 = base + QK_NOPE
        o1, o2 = _rope_rows(qt[r0:r0 + half, :], qt[r0 + half:r0 + QK_ROPE, :], cos, sin)
        qt_ref[0, base:r0, :] = (qt[base:r0, :] * scale).astype(BF16)
        qt_ref[0, r0:r0 + QK_ROPE, :] = (jnp.concatenate([o1, o2], axis=0) * scale).astype(BF16)
        qt_ref[0, r0 + QK_ROPE:base + LANES, :] = jnp.zeros((LANES - QK_NOPE - QK_ROPE, qt.shape[1]), BF16)


def _proj_c(x, g, w, gq, gkv, wuqt, row_tabs, time_tabs, b, t, tm=256):
    m, d = x.shape
    tm = min(tm, t)
    nt = t // tm
    in_specs = [pl.BlockSpec((tm, d), _row), _resident((1, d))]
    in_specs += [_resident(a.shape) for a in w]
    in_specs += [_resident((1, Q_LORA)), _resident((1, KV_LORA)), _resident(wuqt.shape)]
    in_specs += [pl.BlockSpec((tm, LANES), _row_tab(nt)) for _ in row_tabs]
    in_specs += [pl.BlockSpec((tb.shape[0], tm), _time_tab(nt)) for tb in time_tabs]
    nq = H_C * LANES
    return pl.pallas_call(
        _proj_c_kernel,
        grid=(m // tm,),
        in_specs=in_specs,
        out_specs=[_tiled_spec(nq, tm, _tiled_out(nt)), pl.BlockSpec((tm, KV_LORA), _row),
                   pl.BlockSpec((tm, LANES), _row)],
        out_shape=[jax.ShapeDtypeStruct((b, nt, nq, tm), BF16), jax.ShapeDtypeStruct((m, KV_LORA), F32),
                   jax.ShapeDtypeStruct((m, LANES), F32)],
        compiler_params=_cparams(("parallel",)),
        name="proj_c",
    )(x, g.reshape(1, d), *w, gq.reshape(1, Q_LORA), gkv.reshape(1, KV_LORA), wuqt, *row_tabs, *time_tabs)


def _kv_expand_kernel(ckv_ref, kpe_ref, wuk_ref, wuvt_ref, place_ref, k_ref, vt_ref):
    ckv = ckv_ref[0].astype(BF16)
    kpe = _dot(kpe_ref[0].astype(BF16), place_ref[...])
    kn = _dot(ckv, wuk_ref[...])
    for hh in range(H_C):
        sl = slice(hh * LANES, (hh + 1) * LANES)
        k_ref[0, :, sl] = (kn[:, sl] + kpe).astype(BF16)
    _store_aug_values(vt_ref, _dot_nt(wuvt_ref[...], ckv).astype(BF16))


def _kv_expand(ckv, kpe, wuk, wuvt, place, tm):
    b, l, _ = ckv.shape
    rmap = lambda bi, i: (bi, i, 0)
    cmap = lambda bi, i: (0, 0)
    return pl.pallas_call(
        _kv_expand_kernel,
        grid=(b, l // tm),
        in_specs=[pl.BlockSpec((1, tm, KV_LORA), rmap), pl.BlockSpec((1, tm, LANES), rmap),
                  pl.BlockSpec(wuk.shape, cmap), pl.BlockSpec(wuvt.shape, cmap),
                  pl.BlockSpec(place.shape, cmap)],
        out_specs=[pl.BlockSpec((1, tm, H_C * LANES), rmap),
                   _tiled_spec(H_C * VROWS, tm, lambda bi, i: (bi, i, 0, 0))],
        out_shape=[jax.ShapeDtypeStruct((b, l, H_C * LANES), BF16),
                   jax.ShapeDtypeStruct((b, l // tm, H_C * VROWS, tm), BF16)],
        compiler_params=_cparams(("parallel", "parallel")),
        name="kv_expand",
    )(ckv, kpe, wuk, wuvt, place)


def _rope_angles(pos, dim):
    half = dim // 2
    inv = ROPE_THETA ** (-jnp.arange(half, dtype=F32) / half)
    ang = pos.astype(F32)[:, None] * inv[None, :]
    return jnp.cos(ang), jnp.sin(ang)


def _rope_row_tables(pos, dim):
    cos, sin = _rope_angles(pos, dim)
    reps = LANES // dim
    return (jnp.tile(jnp.concatenate([cos, cos], axis=-1), (1, reps)),
            jnp.tile(jnp.concatenate([-sin, sin], axis=-1), (1, reps)))


def _rope_time_tables(pos, dim):
    cos, sin = _rope_angles(pos, dim)
    return cos.T, sin.T


def _prep_weights(p):
    w = {}
    w["ffn1_in"] = p["ffn1_w_in"].astype(BF16)
    w["ffn1_out"] = p["ffn1_w_out"].astype(BF16)
    w["ffn2_in"] = p["ffn2_w_in"].astype(BF16)
    w["ffn2_out"] = p["ffn2_w_out"].astype(BF16)
    a = p["a_w_in"].astype(BF16)
    n = H_A * DH_A
    w["a"] = (a[:, :n].T, a[:, n:2 * n], a[:, 2 * n:3 * n], a[:, 2 * n:3 * n].T, a[:, 3 * n:])
    w["a_out"] = p["a_w_out"].astype(BF16)
    bw = p["b_w_in"].astype(BF16)
    offs = np.cumsum([0, H_B * DH_B, DH_B, DH_B, H_IDX * DH_IDX, DH_IDX, H_IDX]).tolist()
    bq, bk, bv, bqi, bki, bwi = [bw[:, offs[i]:offs[i + 1]] for i in range(6)]
    w["b"] = (bq.T, bk, bv, bv.T, bqi.T, jnp.concatenate([bki, bki], axis=1), bwi.T)
    w["b_out"] = p["b_w_out"].astype(BF16)
    c = p["c_w_down"].astype(BF16)
    d = c.shape[0]
    wkpe = jnp.zeros((d, LANES), BF16).at[:, :QK_ROPE].set(c[:, Q_LORA + KV_LORA:])
    w["c"] = (c[:, :Q_LORA], c[:, Q_LORA:Q_LORA + KV_LORA], wkpe)
    uq = p["c_w_uq"].astype(BF16).reshape(Q_LORA, H_C, QK_NOPE + QK_ROPE)
    uq = jnp.pad(uq, ((0, 0), (0, 0), (0, LANES - QK_NOPE - QK_ROPE)))
    w["c_uqt"] = uq.reshape(Q_LORA, H_C * LANES).T
    ukv = p["c_w_ukv"].astype(BF16).reshape(KV_LORA, H_C, QK_NOPE + V_C)
    uk = jnp.pad(ukv[:, :, :QK_NOPE], ((0, 0), (0, 0), (0, LANES - QK_NOPE)))
    w["c_uk"] = uk.reshape(KV_LORA, H_C * LANES)
    w["c_uvt"] = ukv[:, :, QK_NOPE:].reshape(KV_LORA, H_C * V_C).T
    w["c_place"] = jnp.zeros((LANES, LANES), BF16).at[jnp.arange(QK_ROPE), QK_NOPE + jnp.arange(QK_ROPE)].set(1.0)
    w["c_out"] = p["c_w_out"].astype(BF16)
    dw = p["d_w_in"].astype(BF16)
    n = H_D * DH_D
    w["d"] = (dw[:, :n].T, dw[:, n:2 * n], dw[:, 2 * n:], dw[:, 2 * n:].T)
    w["d_out"] = p["d_w_out"].astype(BF16)
    return w


def _round_up(x, m):
    return (x + m - 1) // m * m


def _cat_time(past, new, lpad, axis):
    a = new if past is None else jnp.concatenate([past.astype(new.dtype), new], axis=axis)
    if a.shape[axis] < lpad:
        pad = [(0, 0)] * a.ndim
        pad[axis] = (0, lpad - a.shape[axis])
        a = jnp.pad(a, pad)
    return a


def _trunk(x3, past, p, w):
    b, t, d = x3.shape
    m = b * t
    x = x3.reshape(m, d)
    past_len = 0 if past is None else past[0][0].shape[1]
    l_valid = past_len + t
    pos = past_len + jnp.arange(t, dtype=jnp.int32)
    if past is None:
        tile = min(256, t)
        tile_s = min(512, t)
        lpad = _round_up(l_valid, tile_s)
        ktile, ktile_s, tpad = tile, tile_s, t
        groups = 8 if t % (8 * tile) == 0 else 1
    else:
        tile = tile_s = tpad = _round_up(t, LANES)
        lpad = ktile_s = _round_up(l_valid, LANES)
        ktile = LANES
        groups = 1
    norm_g = p["norm_g"]
    rows = {}

    def r3(a):
        return a.reshape(b, t, a.shape[-1])

    def q_in(a4):
        return a4 if tpad == t else jnp.pad(a4, ((0, 0), (0, 0), (0, 0), (0, tpad - t)))

    def v_in(vt4, past_t, kt):
        if past is None:
            return vt4
        a = _cat_time(past_t, vt4.reshape(b, vt4.shape[2], t), lpad, 2)
        return jnp.transpose(a.reshape(b, a.shape[1], lpad // kt, kt), (0, 2, 1, 3))

    def unpad(o):
        return (o if tpad == t else o[:, :t]).reshape(m, -1)

    def pst(idx, j, width):
        return None if past is None else past[idx][j].reshape(b, past_len, width)

    def pst_t(idx, j, width):
        return None if past is None else jnp.transpose(past[idx][j].reshape(b, past_len, width), (0, 2, 1))

    def aug(vt_past):
        if vt_past is None:
            return None
        dv = LANES // 2
        v4 = vt_past.reshape(b, -1, dv, past_len)
        extra = jnp.zeros((b, v4.shape[1], VROWS - dv, past_len), v4.dtype).at[:, :, 0, :].set(1.0)
        return jnp.concatenate([v4, extra], axis=2).reshape(b, -1, past_len)

    for i in range(4):
        x = _ffn(x, norm_g[i, 0], w["ffn1_in"][i], w["ffn1_out"][i])
        g = norm_g[i, 1]
        if i == 0:
            wqt, wk, wv, wvt, wf = w["a"]
            qt, k, kb, v, vt, lf = _proj_qkv(x, g, wqt, wk, wv, wvt, DH_A ** -0.5 * LOG2E, b, t, wf, p["a_b_f"],
                                             tm=tile_s)
            rows["a"] = (k.reshape(b, t, H_A, DH_A), v.reshape(b, t, H_A, DH_A), lf.reshape(b, t, H_A))
            lf_all = _cat_time(None if past is None else past[0][2].astype(F32), r3(lf), lpad, 1)
            c = _cumsum_time(jnp.pad(lf_all, ((0, 0), (0, 0), (0, LANES - H_A))))
            o = _attention(q_in(qt), _cat_time(pst(0, 0, H_A * DH_A), r3(kb), lpad, 1),
                           v_in(vt, aug(pst_t(0, 1, H_A * DH_A)), ktile_s), c,
                           nh=H_A, k_per_head=False, chunked=False, q_off=past_len, l_valid=l_valid)
            mix = (unpad(o), w["a_out"])
        elif i == 1:
            row_tabs = list(_rope_row_tables(pos, DH_B)) + list(_rope_row_tables(pos, DH_IDX))
            time_tabs = list(_rope_time_tables(pos, DH_B)) + list(_rope_time_tables(pos, DH_IDX))
            qt, k, kb, v, vt, qit, ki, kib, wit = _proj_b(x, g, w["b"], row_tabs, time_tabs, b, t, tm=tile)
            rows["b"] = (r3(k), r3(v), r3(ki))
            n_sel = min(IDX_TOPK, l_valid // 4)
            o = _dsa_attention(q_in(qt), q_in(qit), q_in(wit), _cat_time(pst(1, 0, DH_B), r3(kb), lpad, 1),
                               _cat_time(pst_t(1, 1, DH_B), vt, lpad, 2),
                               _cat_time(pst(1, 2, DH_IDX), r3(kib), lpad, 1),
                               l_valid=l_valid, q_off=past_len, n_sel=n_sel, groups=groups)
            mix = (unpad(o), w["b_out"])
        elif i == 2:
            row_tabs = list(_rope_row_tables(pos, QK_ROPE))
            time_tabs = list(_rope_time_tables(pos, QK_ROPE))
            qt, ckv, kpe = _proj_c(x, g, w["c"], p["c_g_q"], p["c_g_kv"], w["c_uqt"], row_tabs, time_tabs, b, t,
                                   tm=tile_s)
            rows["c"] = (r3(ckv), r3(kpe)[:, :, :QK_ROPE])
            kpe_past = None if past is None else jnp.pad(past[2][1].astype(F32),
                                                         ((0, 0), (0, 0), (0, LANES - QK_ROPE)))
            ckv_all = _cat_time(None if past is None else past[2][0].astype(F32), r3(ckv), lpad, 1)
            kpe_all = _cat_time(kpe_past, r3(kpe), lpad, 1)
            kc, vct = _kv_expand(ckv_all, kpe_all, w["c_uk"], w["c_uvt"], w["c_place"], tm=ktile_s)
            o = _attention(q_in(qt), kc, vct, None,
                           nh=H_C, k_per_head=True, chunked=True, q_off=past_len, l_valid=l_valid)
            mix = (unpad(o), w["c_out"])
        else:
            wqt, wk, wv, wvt = w["d"]
            qt, k, kb, v, vt = _proj_qkv(x, g, wqt, wk, wv, wvt, DH_D ** -0.5 * LOG2E, b, t, tm=tile)
            rows["d"] = (k.reshape(b, t, H_D, DH_D), v.reshape(b, t, H_D, DH_D))
            o = _sb_attention(q_in(qt), _cat_time(pst(3, 0, H_D * DH_D), r3(kb), lpad, 1),
                              v_in(vt, pst_t(3, 1, H_D * DH_D), ktile), nh=H_D, q_off=past_len)
            mix = (unpad(o), w["d_out"])
        x = _ffn(x, norm_g[i, 2], w["ffn2_in"][i], w["ffn2_out"][i], mix, p["final_g"] if i == 3 else None)
    return x.reshape(b, t, d), rows


def kernel(x_prompt, x_sample, cache_a_k, cache_a_v, cache_a_logf, cache_b_k, cache_b_v, cache_b_kidx,
           cache_c_ckv, cache_c_kpe, cache_d_k, cache_d_v, norm_g, final_g,
           ffn1_w_in, ffn1_w_out, ffn2_w_in, ffn2_w_out, a_w_in, a_b_f, a_w_out, b_w_in, b_w_out,
           c_w_down, c_g_q, c_g_kv, c_w_uq, c_w_ukv, c_w_out, d_w_in, d_w_out):
    p = dict(norm_g=norm_g, final_g=final_g, ffn1_w_in=ffn1_w_in, ffn1_w_out=ffn1_w_out,
             ffn2_w_in=ffn2_w_in, ffn2_w_out=ffn2_w_out, a_w_in=a_w_in, a_b_f=a_b_f, a_w_out=a_w_out,
             b_w_in=b_w_in, b_w_out=b_w_out, c_w_down=c_w_down, c_g_q=c_g_q, c_g_kv=c_g_kv,
             c_w_uq=c_w_uq, c_w_ukv=c_w_ukv, c_w_out=c_w_out, d_w_in=d_w_in, d_w_out=d_w_out)
    w = _prep_weights(p)
    y_prompt, pr = _trunk(x_prompt, None, p, w)
    past = ((cache_a_k, cache_a_v, cache_a_logf), (cache_b_k, cache_b_v, cache_b_kidx),
            (cache_c_ckv, cache_c_kpe), (cache_d_k, cache_d_v))
    y_sample, sr = _trunk(x_sample, past, p, w)
    return (y_prompt, y_sample, *pr["a"], *pr["b"], *pr["c"], *pr["d"],
            *sr["a"], *sr["b"], *sr["c"], *sr["d"])
```
